```python
import math
import jax
import jax.numpy as jnp
from jax import lax
import numpy as np

D_MODEL = 1024
BATCH = 16
SEQ = 4096
DEPTH = 1
DEC_BATCH = 128
DEC_SEQ = 1
PAST_LEN = 8192
PAGE_SIZE = 128

HEAD_DIM = 64
ATTN_PATTERNS = ((128, 1), (512, 4), (2048, 16))
N_PATTERNS = len(ATTN_PATTERNS)
ATTN_HG = 8
WIN_KEYS = 128
ATTN_QKV = N_PATTERNS * 3 * ATTN_HG * HEAD_DIM
ATTN_OUT = ATTN_HG * HEAD_DIM
ROPE_THETA = 10000.0
ATTN_SCALE = HEAD_DIM ** -0.5
NEG_INF = -1e30
SSD_INNER = D_MODEL
SSD_HEADDIM = 64
SSD_HEADS = SSD_INNER // SSD_HEADDIM
SSD_GROUPS = 2
SSD_HPG = SSD_HEADS // SSD_GROUPS
SSD_STATE = 128
SSD_CONV = 4
SSD_CONV_DIM = SSD_INNER + 2 * SSD_GROUPS * SSD_STATE
SSD_CHUNK = 128
MIX_IN = ATTN_QKV + SSD_INNER + SSD_CONV_DIM + SSD_HEADS
MIX_OUT = ATTN_OUT + SSD_INNER
FFN_HIDDEN = -(-8 * D_MODEL // (3 * 256)) * 256
NORM_EPS = 1e-6

kernel_name = 'hymba_ssd_dilated_swa_decoder_step'


def rmsnorm(x, w):
    xf = x.astype(jnp.float32)
    y = xf * lax.rsqrt(jnp.mean(xf * xf, axis=-1, keepdims=True) + NORM_EPS)
    return (y * w.astype(jnp.float32)).astype(x.dtype)


def rope(x, pos):
    half = HEAD_DIM // 2
    inv_freq = ROPE_THETA ** (-jnp.arange(half, dtype=jnp.float32) / half)
    ang = pos.astype(jnp.float32)[:, None] * inv_freq[None, :]
    cos = jnp.cos(ang)[:, None, :]
    sin = jnp.sin(ang)[:, None, :]
    xf = x.astype(jnp.float32)
    x1, x2 = xf[..., :half], xf[..., half:]
    return jnp.concatenate([x1 * cos - x2 * sin, x2 * cos + x1 * sin], axis=-1).astype(x.dtype)


def split_mix(proj):
    lead = proj.shape[:-1]
    o1 = ATTN_QKV
    o2 = o1 + SSD_INNER
    o3 = o2 + SSD_CONV_DIM
    qkv = proj[..., :o1].reshape(lead + (N_PATTERNS, 3, ATTN_HG, HEAD_DIM))
    return qkv, proj[..., o1:o2], proj[..., o2:o3], proj[..., o3:]


def dilated_window_prompt(q, k, v, dil):
    b, s, h, d = q.shape
    L = s // dil
    nb = -(-L // WIN_KEYS)
    Lp = nb * WIN_KEYS

    def to_sub(t):
        t = t.reshape(b, L, dil, h, d).transpose(0, 2, 1, 3, 4)
        t = jnp.pad(t, ((0, 0), (0, 0), (0, Lp - L), (0, 0), (0, 0)))
        return t.reshape(b, dil, nb, WIN_KEYS, h, d)

    def with_prev(t):
        prev = jnp.pad(t, ((0, 0), (0, 0), (1, 0), (0, 0), (0, 0), (0, 0)))[:, :, :-1]
        return jnp.concatenate([prev, t], axis=3)

    qs = to_sub(q)
    kb = with_prev(to_sub(k))
    vb = with_prev(to_sub(v))
    qi = jnp.arange(WIN_KEYS)[:, None]
    ki = jnp.arange(2 * WIN_KEYS)[None, :]
    dist = qi + WIN_KEYS - ki
    blk = jnp.arange(nb)[:, None, None]
    valid = (dist >= 0) & (dist <= WIN_KEYS) & (blk * WIN_KEYS + ki - WIN_KEYS >= 0)
    sc = jnp.einsum('brnqhd,brnkhd->brnhqk', qs, kb, preferred_element_type=jnp.float32) * ATTN_SCALE
    sc = jnp.where(valid[None, None, :, None], sc, NEG_INF)
    m = jnp.max(sc, axis=-1, keepdims=True)
    p = jnp.exp(sc - m)
    den = jnp.sum(p, axis=-1, keepdims=True)
    o = jnp.einsum('brnhqk,brnkhd->brnhqd', p, vb.astype(jnp.float32)) / den
    lse = (m + jnp.log(den))[..., 0]
    o = o.transpose(0, 1, 2, 4, 3, 5).reshape(b, dil, Lp, h, d)[:, :, :L]
    o = o.transpose(0, 2, 1, 3, 4).reshape(b, s, h, d)
    lse = lse.transpose(0, 1, 2, 4, 3).reshape(b, dil, Lp, h)[:, :, :L]
    lse = lse.transpose(0, 2, 1, 3).reshape(b, s, h)
    return o, lse


def dilated_window_sample(q, k_new, v_new, kv_cache, window, dil):
    wb = kv_cache.shape[1]
    t = q.shape[1]
    kv_all = jnp.concatenate([kv_cache.astype(k_new.dtype), jnp.stack([k_new, v_new], axis=2)], axis=1)
    n_keys = window // dil + 1
    idx = wb + jnp.arange(t)[:, None] - dil * jnp.arange(n_keys)[None, :]
    valid = idx >= 0
    sel = kv_all[:, jnp.maximum(idx, 0)]
    sc = jnp.einsum('bthd,btkhd->bthk', q, sel[:, :, :, 0], preferred_element_type=jnp.float32) * ATTN_SCALE
    sc = jnp.where(valid[None, :, None, :], sc, NEG_INF)
    m = jnp.max(sc, axis=-1, keepdims=True)
    p = jnp.exp(sc - m)
    den = jnp.sum(p, axis=-1, keepdims=True)
    o = jnp.einsum('bthk,btkhd->bthd', p, sel[:, :, :, 1].astype(jnp.float32)) / den
    lse = (m + jnp.log(den))[..., 0]
    return o, lse, kv_all[:, -wb:]


def merge_patterns(outs, lses):
    alpha = jax.nn.softmax(jnp.stack(lses, axis=0), axis=0)
    o = sum(alpha[g][..., None] * outs[g] for g in range(N_PATTERNS))
    return o.reshape(o.shape[:-2] + (ATTN_OUT,))


def causal_conv_silu(xpad, conv_w, conv_b, t):
    xf = xpad.astype(jnp.float32)
    w = conv_w.astype(jnp.float32)
    out = conv_b.astype(jnp.float32) + sum(xf[:, j:j + t] * w[j] for j in range(SSD_CONV))
    return jax.nn.silu(out)


def ssd_branch_inputs(xbc_conv, dt_raw, dt_bias, a_log):
    lead = xbc_conv.shape[:-1]
    gn = SSD_GROUPS * SSD_STATE
    xs = xbc_conv[..., :SSD_INNER].reshape(lead + (SSD_HEADS, SSD_HEADDIM))
    Bm = xbc_conv[..., SSD_INNER:SSD_INNER + gn].reshape(lead + (SSD_GROUPS, SSD_STATE))
    Cm = xbc_conv[..., SSD_INNER + gn:].reshape(lead + (SSD_GROUPS, SSD_STATE))
    dt = jax.nn.softplus(dt_raw.astype(jnp.float32) + dt_bias.astype(jnp.float32))
    A = -jnp.exp(a_log.astype(jnp.float32))
    return xs, Bm, Cm, dt, A


def ssd_chunked(xs, dt, A, Bm, Cm):
    b, s = xs.shape[:2]
    nc = s // SSD_CHUNK
    xc = xs.reshape(b, nc, SSD_CHUNK, SSD_GROUPS, SSD_HPG, SSD_HEADDIM)
    dtc = dt.reshape(b, nc, SSD_CHUNK, SSD_GROUPS, SSD_HPG)
    Bc = Bm.reshape(b, nc, SSD_CHUNK, SSD_GROUPS, SSD_STATE)
    Cc = Cm.reshape(b, nc, SSD_CHUNK, SSD_GROUPS, SSD_STATE)
    cs = jnp.cumsum(dtc * A.reshape(SSD_GROUPS, SSD_HPG), axis=2)
    causal = jnp.tril(jnp.ones((SSD_CHUNK, SSD_CHUNK), dtype=bool))[:, :, None, None]
    seg = jnp.where(causal, cs[:, :, :, None] - cs[:, :, None, :], NEG_INF)
    wgt = jnp.einsum('bctgn,bcsgn->bctsg', Cc, Bc)[..., None] * jnp.exp(seg) * dtc[:, :, None]
    y_diag = jnp.einsum('bctsgh,bcsghp->bctghp', wgt, xc)
    decay_end = jnp.exp(cs[:, :, -1:] - cs)
    states = jnp.einsum('bclgn,bclgh,bclghp->bcghpn', Bc, decay_end * dtc, xc)
    chunk_decay = jnp.exp(cs[:, :, -1])

    def step(h, inp):
        st, dec = inp
        return h * dec[..., None, None] + st, h

    h0 = jnp.zeros((b, SSD_GROUPS, SSD_HPG, SSD_HEADDIM, SSD_STATE), jnp.float32)
    h_last, h_in = lax.scan(step, h0, (jnp.moveaxis(states, 1, 0), jnp.moveaxis(chunk_decay, 1, 0)))
    h_in = jnp.moveaxis(h_in, 0, 1)
    y_off = jnp.einsum('bclgn,bcghpn,bclgh->bclghp', Cc, h_in, jnp.exp(cs))
    y = (y_diag + y_off).reshape(b, s, SSD_HEADS, SSD_HEADDIM)
    return y, h_last.reshape(b, SSD_HEADS, SSD_HEADDIM, SSD_STATE)


def ssd_recurrent(xs, dt, A, Bm, Cm, h0):
    Bh = jnp.repeat(Bm, SSD_HPG, axis=2)
    Ch = jnp.repeat(Cm, SSD_HPG, axis=2)

    def step(h, inp):
        xt, dtt, bt, ct = inp
        h = h * jnp.exp(dtt * A)[:, :, None, None] + jnp.einsum('bhp,bhn->bhpn', dtt[:, :, None] * xt, bt)
        return h, jnp.einsum('bhpn,bhn->bhp', h, ct)

    seq_in = (jnp.swapaxes(xs, 0, 1), jnp.swapaxes(dt, 0, 1), jnp.swapaxes(Bh, 0, 1), jnp.swapaxes(Ch, 0, 1))
    h, ys = lax.scan(step, h0.astype(jnp.float32), seq_in)
    return jnp.swapaxes(ys, 0, 1), h


def ssd_gate_out(y, xs, z, d_skip, ssd_norm_w):
    y = y + d_skip.astype(jnp.float32)[:, None] * xs
    y = y.reshape(y.shape[:-2] + (SSD_INNER,)) * jax.nn.silu(z.astype(jnp.float32))
    return rmsnorm(y, ssd_norm_w)


def mixer_prompt(xn, w_in, w_out, conv_w, conv_b, dt_bias, a_log, d_skip, ssd_norm_w):
    b, s, _ = xn.shape
    qkv, z, xbc, dt_raw = split_mix(xn @ w_in)
    pos = jnp.arange(s)
    outs, lses, kv_new = [], [], []
    for g, (window, dil) in enumerate(ATTN_PATTERNS):
        q = rope(qkv[..., g, 0, :, :], pos)
        k = rope(qkv[..., g, 1, :, :], pos)
        v = qkv[..., g, 2, :, :]
        o, lse = dilated_window_prompt(q, k, v, dil)
        outs.append(o)
        lses.append(lse)
        kv_new.append(jnp.stack([k, v], axis=2)[:, s - min(window, s):])
    attn = merge_patterns(outs, lses)
    xbc_pad = jnp.pad(xbc, ((0, 0), (SSD_CONV - 1, 0), (0, 0)))
    conv_new = xbc_pad[:, s:]
    xs, Bm, Cm, dt, A = ssd_branch_inputs(causal_conv_silu(xbc_pad, conv_w, conv_b, s), dt_raw, dt_bias, a_log)
    y, h_new = ssd_chunked(xs, dt, A, Bm, Cm)
    ssm = ssd_gate_out(y, xs, z, d_skip, ssd_norm_w)
    mix = jnp.concatenate([attn.astype(xn.dtype), ssm.astype(xn.dtype)], axis=-1) @ w_out
    return mix, kv_new, conv_new, h_new


def mixer_sample(xn, kv_caches, conv_state, ssm_state, w_in, w_out, conv_w, conv_b, dt_bias, a_log, d_skip, ssd_norm_w):
    b, t, _ = xn.shape
    qkv, z, xbc, dt_raw = split_mix(xn @ w_in)
    pos = PAST_LEN + jnp.arange(t)
    outs, lses, kv_new = [], [], []
    for g, (window, dil) in enumerate(ATTN_PATTERNS):
        q = rope(qkv[..., g, 0, :, :], pos)
        k = rope(qkv[..., g, 1, :, :], pos)
        v = qkv[..., g, 2, :, :]
        o, lse, kv_upd = dilated_window_sample(q, k, v, kv_caches[g], window, dil)
        outs.append(o)
        lses.append(lse)
        kv_new.append(kv_upd)
    attn = merge_patterns(outs, lses)
    xbc_cat = jnp.concatenate([conv_state.astype(xbc.dtype), xbc], axis=1)
    conv_new = xbc_cat[:, -(SSD_CONV - 1):]
    xs, Bm, Cm, dt, A = ssd_branch_inputs(causal_conv_silu(xbc_cat, conv_w, conv_b, t), dt_raw, dt_bias, a_log)
    y, h_new = ssd_recurrent(xs, dt, A, Bm, Cm, ssm_state)
    ssm = ssd_gate_out(y, xs, z, d_skip, ssd_norm_w)
    mix = jnp.concatenate([attn.astype(xn.dtype), ssm.astype(xn.dtype)], axis=-1) @ w_out
    return mix, kv_new, conv_new, h_new


def swiglu(x, w_ffn_in, w_ffn_out):
    hcat = x @ w_ffn_in
    gate, up = hcat[..., :FFN_HIDDEN], hcat[..., FFN_HIDDEN:]
    return (jax.nn.silu(gate) * up) @ w_ffn_out


def setup_inputs(seed: int = 0) -> dict:
    key = jax.random.key(seed)
    ks = jax.random.split(key, 24)
    f32 = jnp.float32

    def nrm(k, shape, scale):
        return jax.random.normal(k, shape, f32) * scale

    def gain(k, n):
        return 1.0 + 0.01 * jax.random.normal(k, (DEPTH, n), f32)

    wb = [min(w, PAST_LEN) for (w, _) in ATTN_PATTERNS]
    dt0 = jnp.exp(jax.random.uniform(ks[13], (DEPTH, SSD_HEADS), f32, math.log(1e-3), math.log(1e-1)))
    return {
        'x_prompt': nrm(ks[0], (BATCH, SEQ, D_MODEL), 1.0),
        'x_sample': nrm(ks[1], (DEC_BATCH, DEC_SEQ, D_MODEL), 1.0),
        'cache_kv_w128': nrm(ks[2], (DEPTH, DEC_BATCH, wb[0], 2, ATTN_HG, HEAD_DIM), 1.0),
        'cache_kv_w512': nrm(ks[3], (DEPTH, DEC_BATCH, wb[1], 2, ATTN_HG, HEAD_DIM), 1.0),
        'cache_kv_w2048': nrm(ks[4], (DEPTH, DEC_BATCH, wb[2], 2, ATTN_HG, HEAD_DIM), 1.0),
        'state_conv': nrm(ks[5], (DEPTH, DEC_BATCH, SSD_CONV - 1, SSD_CONV_DIM), 1.0),
        'state_ssm': nrm(ks[6], (DEPTH, DEC_BATCH, SSD_HEADS, SSD_HEADDIM, SSD_STATE), 0.5),
        'norm_mix_pre': gain(ks[7], D_MODEL),
        'norm_mix_post': gain(ks[8], D_MODEL),
        'norm_ffn_pre': gain(ks[9], D_MODEL),
        'norm_ffn_post': gain(ks[10], D_MODEL),
        'w_in': nrm(ks[11], (DEPTH, D_MODEL, MIX_IN), D_MODEL ** -0.5),
        'w_out': nrm(ks[12], (DEPTH, MIX_OUT, D_MODEL), MIX_OUT ** -0.5),
        'conv_w': nrm(ks[14], (DEPTH, SSD_CONV, SSD_CONV_DIM), SSD_CONV ** -0.5),
        'conv_b': nrm(ks[15], (DEPTH, SSD_CONV_DIM), 0.02),
        'dt_bias': dt0 + jnp.log(-jnp.expm1(-dt0)),
        'a_log': jnp.log(jax.random.uniform(ks[16], (DEPTH, SSD_HEADS), f32, 1.0, 16.0)),
        'd_skip': 1.0 + 0.01 * jax.random.normal(ks[17], (DEPTH, SSD_HEADS), f32),
        'ssd_norm_w': gain(ks[18], SSD_INNER),
        'w_ffn_in': nrm(ks[19], (DEPTH, D_MODEL, 2 * FFN_HIDDEN), D_MODEL ** -0.5),
        'w_ffn_out': nrm(ks[20], (DEPTH, FFN_HIDDEN, D_MODEL), FFN_HIDDEN ** -0.5),
    }


def reference(x_prompt, x_sample, cache_kv_w128, cache_kv_w512, cache_kv_w2048, state_conv, state_ssm,
              norm_mix_pre, norm_mix_post, norm_ffn_pre, norm_ffn_post, w_in, w_out, conv_w, conv_b,
              dt_bias, a_log, d_skip, ssd_norm_w, w_ffn_in, w_ffn_out):
    yp, ys = x_prompt, x_sample
    p_kv = ([], [], [])
    s_kv = ([], [], [])
    p_conv, p_ssm, s_conv, s_ssm = [], [], [], []
    for l in range(DEPTH):
        wl = (w_in[l], w_out[l], conv_w[l], conv_b[l], dt_bias[l], a_log[l], d_skip[l], ssd_norm_w[l])
        mix, kvs, cst, hst = mixer_prompt(rmsnorm(yp, norm_mix_pre[l]), *wl)
        yp = yp + rmsnorm(mix, norm_mix_post[l])
        yp = yp + rmsnorm(swiglu(rmsnorm(yp, norm_ffn_pre[l]), w_ffn_in[l], w_ffn_out[l]), norm_ffn_post[l])
        for g in range(N_PATTERNS):
            p_kv[g].append(kvs[g])
        p_conv.append(cst)
        p_ssm.append(hst)
        caches = (cache_kv_w128[l], cache_kv_w512[l], cache_kv_w2048[l])
        mix, kvs, cst, hst = mixer_sample(rmsnorm(ys, norm_mix_pre[l]), caches, state_conv[l], state_ssm[l], *wl)
        ys = ys + rmsnorm(mix, norm_mix_post[l])
        ys = ys + rmsnorm(swiglu(rmsnorm(ys, norm_ffn_pre[l]), w_ffn_in[l], w_ffn_out[l]), norm_ffn_post[l])
        for g in range(N_PATTERNS):
            s_kv[g].append(kvs[g])
        s_conv.append(cst)
        s_ssm.append(hst)
    p_kv_w128 = jnp.stack(p_kv[0])
    p_kv_w512 = jnp.stack(p_kv[1])
    p_kv_w2048 = jnp.stack(p_kv[2])
    p_conv_new = jnp.stack(p_conv)
    p_ssm_new = jnp.stack(p_ssm)
    s_kv_w128 = jnp.stack(s_kv[0])
    s_kv_w512 = jnp.stack(s_kv[1])
    s_kv_w2048 = jnp.stack(s_kv[2])
    s_conv_new = jnp.stack(s_conv)
    s_ssm_new = jnp.stack(s_ssm)
    return (yp, ys, p_kv_w128, p_kv_w512, p_kv_w2048, p_conv_new, p_ssm_new,
            s_kv_w128, s_kv_w512, s_kv_w2048, s_conv_new, s_ssm_new)
```

```python
import functools

import jax
import jax.numpy as jnp
from jax import lax
from jax.experimental import pallas as pl
from jax.experimental.pallas import tpu as pltpu

F32 = jnp.float32
BF16 = jnp.bfloat16

D_MODEL = 1024
HEAD_DIM = 64
ATTN_PATTERNS = ((128, 1), (512, 4), (2048, 16))
N_PATTERNS = len(ATTN_PATTERNS)
ATTN_HG = 8
WIN_KEYS = 128
ATTN_OUT = ATTN_HG * HEAD_DIM
ATTN_QKV = N_PATTERNS * 3 * ATTN_OUT
ROPE_THETA = 10000.0
ATTN_SCALE = HEAD_DIM ** -0.5
NEG_INF = -1e30
SSD_INNER = D_MODEL
SSD_HEADDIM = 64
SSD_HEADS = SSD_INNER // SSD_HEADDIM
SSD_GROUPS = 2
SSD_HPG = SSD_HEADS // SSD_GROUPS
SSD_STATE = 128
SSD_CONV = 4
SSD_CONV_DIM = SSD_INNER + 2 * SSD_GROUPS * SSD_STATE
FFN_HIDDEN = 2816
NORM_EPS = 1e-6

LANES = 128
SUBLANES = 8
ROW_TILE = 512
SSD_T = 128
FFN_CHUNK = 256
VMEM_LIMIT = 56 * 1024 * 1024


def _cparams(sem):
    return pltpu.CompilerParams(dimension_semantics=sem, vmem_limit_bytes=VMEM_LIMIT)


def _const_spec(shape):
    nd = len(shape)
    return pl.BlockSpec(shape, lambda *_: (0,) * nd, pipeline_mode=pl.Buffered(1))


def _split2(x):
    hi = x.astype(BF16)
    lo = (x - hi.astype(F32)).astype(BF16)
    return hi, lo


def _split3(x):
    hi = x.astype(BF16)
    r = x - hi.astype(F32)
    mid = r.astype(BF16)
    lo = (r - mid.astype(F32)).astype(BF16)
    return hi, mid, lo


def _sigmoid(x):
    return 1.0 / (1.0 + jnp.exp(-x))


def _softplus(x):
    return jnp.maximum(x, 0.0) + jnp.log1p(jnp.exp(-jnp.abs(x)))


def _rms_scale(x):
    return x * lax.rsqrt(jnp.mean(x * x, axis=-1, keepdims=True) + NORM_EPS)


def _inproj_kernel(x_ref, nw_ref, cos_ref, sin_ref, wqkv_ref, wz_ref, wxbc_ref, wdt_ref,
                   qkv_ref, z_ref, xbc_ref, dt_ref, xn_ref):
    tm = x_ref.shape[0]
    xn_ref[...] = (_rms_scale(x_ref[...]) * nw_ref[...]).astype(BF16)
    xn = xn_ref[...]
    cos = cos_ref[...]
    sin = sin_ref[...]
    lane = lax.broadcasted_iota(jnp.int32, (tm, LANES), 1)
    first_half = (lane % HEAD_DIM) < (HEAD_DIM // 2)
    n_blk = ATTN_QKV // ATTN_OUT
    for j in range(n_blk):
        acc = jnp.dot(xn, wqkv_ref[:, j * ATTN_OUT:(j + 1) * ATTN_OUT], preferred_element_type=F32)
        for c in range(ATTN_OUT // LANES):
            a = acc[:, c * LANES:(c + 1) * LANES]
            if j % 3 != 2:
                partner = jnp.where(first_half, pltpu.roll(a, LANES - HEAD_DIM // 2, 1),
                                    pltpu.roll(a, HEAD_DIM // 2, 1))
                a = a * cos + partner * sin
                if j % 3 == 0:
                    a = a * ATTN_SCALE
            qkv_ref[:, j * ATTN_OUT + c * LANES:j * ATTN_OUT + (c + 1) * LANES] = a.astype(BF16)
    z_ref[...] = jnp.dot(xn, wz_ref[...], preferred_element_type=F32).astype(BF16)
    xbc_ref[...] = jnp.dot(xn, wxbc_ref[...], preferred_element_type=F32).astype(BF16)
    dt_ref[...] = jnp.dot(xn, wdt_ref[...], preferred_element_type=F32)


def _inproj(x2d, nw, cos_t, sin_t, wqkv, wz, wxbc, wdt):
    m = x2d.shape[0]
    tm = min(ROW_TILE, m)
    tab_blocks = cos_t.shape[0] // tm
    row = lambda w: pl.BlockSpec((tm, w), lambda i: (i, 0))
    tab = pl.BlockSpec((tm, LANES), lambda i: (i % tab_blocks, 0))
    return pl.pallas_call(
        _inproj_kernel,
        grid=(m // tm,),
        in_specs=[row(D_MODEL), _const_spec((1, D_MODEL)), tab, tab,
                  _const_spec(wqkv.shape), _const_spec(wz.shape), _const_spec(wxbc.shape),
                  _const_spec(wdt.shape)],
        out_specs=[row(ATTN_QKV), row(SSD_INNER), row(SSD_CONV_DIM), row(LANES)],
        out_shape=[jax.ShapeDtypeStruct((m, ATTN_QKV), BF16),
                   jax.ShapeDtypeStruct((m, SSD_INNER), BF16),
                   jax.ShapeDtypeStruct((m, SSD_CONV_DIM), BF16),
                   jax.ShapeDtypeStruct((m, LANES), F32)],
        scratch_shapes=[pltpu.VMEM((tm, D_MODEL), BF16)],
        compiler_params=_cparams(("arbitrary",)),
        name="inproj",
    )(x2d, nw, cos_t, sin_t, wqkv, wz, wxbc, wdt)


def _attn_kernel(q_ref, kc_ref, vc_ref, kp_ref, vp_ref, o_ref, lse_ref, k_scr, v_scr, *, group):
    cq = q_ref.shape[1]
    w = WIN_KEYS
    n = pl.program_id(2)
    k_scr[0:w] = kp_ref[0]
    k_scr[w:] = kc_ref[0]
    v_scr[0:w] = vp_ref[0]
    v_scr[w:] = vc_ref[0]
    lane = lax.broadcasted_iota(jnp.int32, (w, LANES), 1)
    lo = lane < HEAD_DIM
    qi = lax.broadcasted_iota(jnp.int32, (w, 2 * w), 0)
    ki = lax.broadcasted_iota(jnp.int32, (w, 2 * w), 1)
    band = ((ki < w) & (ki >= qi)) | ((ki >= w) & ((ki - w) <= qi))
    for i in range(cq // w):
        if i == 0:
            kmin = jnp.where(n > 0, 0, w)
            valid = band & (ki >= kmin)
        else:
            valid = band
        lse_tile = jnp.zeros((w, LANES), F32)
        for pair in range(ATTN_OUT // LANES):
            cols = slice(pair * LANES, (pair + 1) * LANES)
            qp = q_ref[0, i * w:(i + 1) * w, cols]
            kk = k_scr[i * w:(i + 2) * w, cols]
            vv = v_scr[i * w:(i + 2) * w, cols]
            halves = []
            for half in range(2):
                qh = jnp.where(lo if half == 0 else jnp.logical_not(lo), qp, jnp.zeros_like(qp))
                s = lax.dot_general(qh, kk, (((1,), (1,)), ((), ())), preferred_element_type=F32)
                s = jnp.where(valid, s, NEG_INF)
                m = jnp.max(s, axis=-1, keepdims=True)
                p = jnp.exp(s - m)
                den = jnp.sum(p, axis=-1, keepdims=True)
                pv = jnp.dot(p.astype(BF16), vv, preferred_element_type=F32)
                halves.append(pv / den)
                head = 2 * pair + half
                lse_tile = jnp.where(lane == group * ATTN_HG + head, m + jnp.log(den), lse_tile)
            o_ref[0, i * w:(i + 1) * w, cols] = jnp.where(lo, halves[0], halves[1]).astype(BF16)
        lse_ref[0, i * w:(i + 1) * w, :] = lse_tile


def _prompt_attention(qkv, b, s, group):
    dil = ATTN_PATTERNS[group][1]
    sub_len = s // dil
    cq = min(ROW_TILE, sub_len)
    blk_per_chunk = cq // WIN_KEYS
    n_col = ATTN_QKV // ATTN_OUT
    qkv3 = qkv.reshape(b, sub_len, dil * ATTN_QKV)
    cur = lambda t: pl.BlockSpec((1, cq, ATTN_OUT), lambda bi, r, n: (bi, n, r * n_col + 3 * group + t))
    prev = lambda t: pl.BlockSpec(
        (1, WIN_KEYS, ATTN_OUT),
        lambda bi, r, n: (bi, jnp.maximum(n * blk_per_chunk - 1, 0), r * n_col + 3 * group + t))
    o, lse = pl.pallas_call(
        functools.partial(_attn_kernel, group=group),
        grid=(b, dil, sub_len // cq),
        in_specs=[cur(0), cur(1), cur(2), prev(1), prev(2)],
        out_specs=[pl.BlockSpec((1, cq, ATTN_OUT), lambda bi, r, n: (bi, n, r)),
                   pl.BlockSpec((1, cq, LANES), lambda bi, r, n: (bi, n, r))],
        out_shape=[jax.ShapeDtypeStruct((b, sub_len, dil * ATTN_OUT), BF16),
                   jax.ShapeDtypeStruct((b, sub_len, dil * LANES), F32)],
        scratch_shapes=[pltpu.VMEM((cq + WIN_KEYS, ATTN_OUT), BF16),
                        pltpu.VMEM((cq + WIN_KEYS, ATTN_OUT), BF16)],
        compiler_params=_cparams(("arbitrary", "arbitrary", "arbitrary")),
        name=f"prompt_attn_g{group}",
    )(qkv3, qkv3, qkv3, qkv3, qkv3)
    return o.reshape(b * s, ATTN_OUT), lse.reshape(b * s, LANES)


def _pack3_lanes(x):
    hi, mid, lo = _split3(x)
    packed = hi.astype(F32) + pltpu.roll(mid.astype(F32), SSD_HEADS, 1) + pltpu.roll(lo.astype(F32), 2 * SSD_HEADS, 1)
    return packed.astype(BF16)


def _ssd_kernel(xbc_ref, z_ref, dtr_ref, convw_ref, convb_ref, dtb_ref, alog_ref, dskip_ref, nw_ref,
                tri3_ref, rexp_ref, eexp_ref,
                y_ref, pconv_ref, hT_out_ref,
                xpad_ref, hT_ref, csb_ref, ybuf_ref):
    t = SSD_T
    c = pl.program_id(1)
    last = pl.num_programs(1) - 1
    pad = SUBLANES
    taps = SSD_CONV

    @pl.when(c == 0)
    def _():
        xpad_ref[0:pad] = jnp.zeros((pad, SSD_CONV_DIM), F32)
        hT_ref[...] = jnp.zeros_like(hT_ref)

    xpad_ref[pad:pad + t] = xbc_ref[0].astype(F32)
    acc = convb_ref[...]
    for j in range(taps):
        off = pad - (taps - 1) + j
        acc = acc + convw_ref[j:j + 1, :] * xpad_ref[off:off + t]

    @pl.when(c == last)
    def _():
        pconv_ref[0] = xpad_ref[pad + t - (taps - 1):pad + t]

    xpad_ref[0:pad] = xpad_ref[t:t + pad]
    xc = acc * _sigmoid(acc)
    xs = xc[:, :SSD_INNER]
    xs_b = xs.astype(BF16)
    gn = SSD_GROUPS * SSD_STATE

    lane = lax.broadcasted_iota(jnp.int32, (t, LANES), 1)
    dt = jnp.where(lane < SSD_HEADS, _softplus(dtr_ref[0] + dtb_ref[...]), 0.0)
    a = dt * (-jnp.exp(alog_ref[...]))
    a3 = jnp.concatenate(_split3(a), axis=0)
    cs = jnp.dot(tri3_ref[...], a3, preferred_element_type=F32)
    cs_last = cs[t - 1:t, :]
    wl = jnp.exp(cs_last - cs) * dt
    csT = cs.T
    dtT = dt.T
    wlT = wl.T
    csb_ref[...] = jnp.dot(_pack3_lanes(cs), rexp_ref[...], preferred_element_type=F32)
    cs_last8 = jnp.broadcast_to(cs_last, (SUBLANES, LANES))
    dec = jnp.exp(jnp.dot(_pack3_lanes(cs_last8), eexp_ref[...], preferred_element_type=F32)[0:1, :])

    ti = lax.broadcasted_iota(jnp.int32, (t, t), 0)
    si = lax.broadcasted_iota(jnp.int32, (t, t), 1)
    tri = si <= ti
    lo = lane < SSD_HEADDIM

    g_mat, bT, yoff = [], [], []
    for g in range(SSD_GROUPS):
        bm = xc[:, SSD_INNER + g * SSD_STATE:SSD_INNER + (g + 1) * SSD_STATE]
        cm = xc[:, SSD_INNER + gn + g * SSD_STATE:SSD_INNER + gn + (g + 1) * SSD_STATE].astype(BF16)
        g_mat.append(lax.dot_general(cm, bm.astype(BF16), (((1,), (1,)), ((), ())),
                                     preferred_element_type=F32))
        bT.append(bm.T)
        cols = slice(g * SSD_HPG * SSD_HEADDIM, (g + 1) * SSD_HPG * SSD_HEADDIM)
        yoff.append(jnp.dot(cm, hT_ref[:, cols].astype(BF16), preferred_element_type=F32))

    ss = jnp.zeros((t, 1), F32)
    pairs_per_group = SSD_HPG // 2
    for pair in range(SSD_HEADS // 2):
        g = pair // pairs_per_group
        cols = slice(pair * LANES, (pair + 1) * LANES)
        w_blocks, b_blocks = [], []
        for h in (2 * pair, 2 * pair + 1):
            seg = csb_ref[:, h * LANES:(h + 1) * LANES] - csT[h:h + 1, :]
            seg = jnp.where(tri, seg, NEG_INF)
            w_blocks.append((g_mat[g] * jnp.exp(seg) * dtT[h:h + 1, :]).astype(BF16))
            b_blocks.append((bT[g] * wlT[h:h + 1, :]).astype(BF16))
        lhs = jnp.concatenate([jnp.concatenate(w_blocks, axis=1), jnp.concatenate(b_blocks, axis=1)], axis=0)
        xp = xs_b[:, cols]
        zero = jnp.zeros_like(xp)
        xbd = jnp.concatenate([jnp.where(lo, xp, zero), jnp.where(lo, zero, xp)], axis=0)
        res = jnp.dot(lhs, xbd, preferred_element_type=F32)
        ecs = jnp.exp(jnp.where(lo, csb_ref[:, 2 * pair * LANES:(2 * pair + 1) * LANES],
                                csb_ref[:, (2 * pair + 1) * LANES:(2 * pair + 2) * LANES]))
        gcol = (pair % pairs_per_group) * LANES
        y = res[:t] + yoff[g][:, gcol:gcol + LANES] * ecs + dskip_ref[:, cols] * xs[:, cols]
        hT_ref[:, cols] = hT_ref[:, cols] * dec[:, cols] + res[t:]
        zf = z_ref[0, :, cols].astype(F32)
        gated = y * (zf * _sigmoid(zf))
        ybuf_ref[:, cols] = gated
        ss = ss + jnp.sum(gated * gated, axis=-1, keepdims=True)

    y_ref[0] = (ybuf_ref[...] * lax.rsqrt(ss * (1.0 / SSD_INNER) + NORM_EPS) * nw_ref[...]).astype(BF16)

    @pl.when(c == last)
    def _():
        hT_out_ref[0] = hT_ref[...]


def _ssd_consts():
    t = SSD_T
    tri = (jnp.arange(t)[:, None] >= jnp.arange(t)[None, :]).astype(BF16)
    tri3 = jnp.concatenate([tri, tri, tri], axis=1)
    k = jnp.arange(LANES)
    piece_head = jnp.where(k < 3 * SSD_HEADS, k % SSD_HEADS, -1)
    rexp = (piece_head[:, None] == (jnp.arange(SSD_HEADS * LANES) // LANES)[None, :]).astype(BF16)
    eexp = (piece_head[:, None] == (jnp.arange(SSD_INNER) // SSD_HEADDIM)[None, :]).astype(BF16)
    return tri3, rexp, eexp


def _prompt_ssd(xbc, z, dt_raw, conv_w, conv_b, dt_bias128, a_log128, dskip_row, ssd_nw, b, s):
    t = SSD_T
    tri3, rexp, eexp = _ssd_consts()
    tok = lambda w: pl.BlockSpec((1, t, w), lambda bi, c: (bi, c, 0))
    per_b = lambda *shape: pl.BlockSpec((1,) + shape, lambda bi, c: (bi,) + (0,) * len(shape))
    return pl.pallas_call(
        _ssd_kernel,
        grid=(b, s // t),
        in_specs=[tok(SSD_CONV_DIM), tok(SSD_INNER), tok(LANES),
                  _const_spec(conv_w.shape), _const_spec(conv_b.shape), _const_spec(dt_bias128.shape),
                  _const_spec(a_log128.shape), _const_spec(dskip_row.shape), _const_spec(ssd_nw.shape),
                  _const_spec(tri3.shape), _const_spec(rexp.shape), _const_spec(eexp.shape)],
        out_specs=[tok(SSD_INNER), per_b(SSD_CONV - 1, SSD_CONV_DIM), per_b(SSD_STATE, SSD_INNER)],
        out_shape=[jax.ShapeDtypeStruct((b, s, SSD_INNER), BF16),
                   jax.ShapeDtypeStruct((b, SSD_CONV - 1, SSD_CONV_DIM), F32),
                   jax.ShapeDtypeStruct((b, SSD_STATE, SSD_INNER), F32)],
        scratch_shapes=[pltpu.VMEM((SUBLANES + t, SSD_CONV_DIM), F32),
                        pltpu.VMEM((SSD_STATE, SSD_INNER), F32),
                        pltpu.VMEM((t, SSD_HEADS * LANES), F32),
                        pltpu.VMEM((t, SSD_INNER), F32)],
        compiler_params=_cparams(("arbitrary", "arbitrary")),
        name="prompt_ssd",
    )(xbc.reshape(b, s, SSD_CONV_DIM), z.reshape(b, s, SSD_INNER), dt_raw.reshape(b, s, LANES),
      conv_w, conv_b, dt_bias128, a_log128, dskip_row, ssd_nw, tri3, rexp, eexp)


def _sample_attn_kernel(q_ref, c0_ref, c1_ref, c2_ref, o_ref, lse_ref):
    bb = q_ref.shape[0]
    caches = (c0_ref, c1_ref, c2_ref)
    for g in range(N_PATTERNS):
        for bi in range(bb):
            q = q_ref[bi, 3 * g]
            k_new = q_ref[bi, 3 * g + 1]
            v_new = q_ref[bi, 3 * g + 2]
            keys = caches[g][bi, :, 0]
            vals = caches[g][bi, :, 1]
            sc = jnp.sum(keys * q[None], axis=-1, keepdims=True)
            sc_new = jnp.sum(k_new * q, axis=-1, keepdims=True)
            m = jnp.maximum(jnp.max(sc, axis=0), sc_new)
            p = jnp.exp(sc - m[None])
            p_new = jnp.exp(sc_new - m)
            den = jnp.sum(p, axis=0) + p_new
            num = jnp.sum(p * vals, axis=0) + p_new * v_new
            o_ref[bi, g] = num / den
            lse_ref[bi, g] = jnp.broadcast_to(m + jnp.log(den), (ATTN_HG, HEAD_DIM))


def _sample_attention(qkv_s, caches):
    db = qkv_s.shape[0]
    bb = 4
    q4 = qkv_s.astype(F32).reshape(db, 3 * N_PATTERNS, ATTN_HG, HEAD_DIM)
    views, specs = [], []
    for cache, (window, dil) in zip(caches, ATTN_PATTERNS):
        wb = cache.shape[1]
        assert wb == window, "sample path expects a full window of cached rows"
        views.append(cache.reshape(db, wb // dil, dil, 2, ATTN_HG, HEAD_DIM))
        specs.append(pl.BlockSpec((bb, WIN_KEYS, None, 2, ATTN_HG, HEAD_DIM), lambda i: (i, 0, 0, 0, 0, 0)))
    small = lambda n: pl.BlockSpec((bb, n, ATTN_HG, HEAD_DIM), lambda i: (i, 0, 0, 0))
    o, lse = pl.pallas_call(
        _sample_attn_kernel,
        grid=(db // bb,),
        in_specs=[small(3 * N_PATTERNS)] + specs,
        out_specs=[small(N_PATTERNS), small(N_PATTERNS)],
        out_shape=[jax.ShapeDtypeStruct((db, N_PATTERNS, ATTN_HG, HEAD_DIM), F32)] * 2,
        compiler_params=_cparams(("arbitrary",)),
        name="sample_attn",
    )(q4, *views)
    outs = [o[:, g].reshape(db, ATTN_OUT).astype(BF16) for g in range(N_PATTERNS)]
    lses = [jnp.pad(lse[:, g, :, 0], ((0, 0), (g * ATTN_HG, LANES - (g + 1) * ATTN_HG))) for g in range(N_PATTERNS)]
    return outs, lses


def _sample_ssd_kernel(xbc_ref, z_ref, dtr_ref, sconv_ref, h_ref,
                       convw_ref, convb_ref, dtb_ref, alog_ref, dskip_ref, nw_ref, eexp_ref,
                       y_ref, sconv_out_ref, h_out_ref,
                       uT_ref, decT_ref, bm_ref, cT_ref, xs_ref, yT_ref):
    db = xbc_ref.shape[0]
    bb = h_ref.shape[0]
    i = pl.program_id(0)
    half = SSD_HPG * SSD_HEADDIM
    gn = SSD_GROUPS * SSD_STATE
    cd = SSD_CONV_DIM

    @pl.when(i == 0)
    def _():
        new = xbc_ref[...].astype(F32)
        acc = convb_ref[...] + convw_ref[SSD_CONV - 1:SSD_CONV, :] * new
        for j in range(SSD_CONV - 1):
            acc = acc + convw_ref[j:j + 1, :] * sconv_ref[:, j * cd:(j + 1) * cd]
        for j in range(1, SSD_CONV - 1):
            sconv_out_ref[:, (j - 1) * cd:j * cd] = sconv_ref[:, j * cd:(j + 1) * cd]
        sconv_out_ref[:, (SSD_CONV - 2) * cd:] = new
        xc = acc * _sigmoid(acc)
        xs = xc[:, :SSD_INNER]
        xs_ref[...] = xs
        lane = lax.broadcasted_iota(jnp.int32, (db, LANES), 1)
        dt = jnp.where(lane < SSD_HEADS, _softplus(dtr_ref[...] + dtb_ref[...]), 0.0)
        dec = jnp.exp(dt * (-jnp.exp(alog_ref[...])))
        dt_hi, dt_lo = _split2(dt)
        dt_e = (jnp.dot(dt_hi, eexp_ref[...], preferred_element_type=F32)
                + jnp.dot(dt_lo, eexp_ref[...], preferred_element_type=F32))
        dc_hi, dc_lo = _split2(dec)
        dec_e = (jnp.dot(dc_hi, eexp_ref[...], preferred_element_type=F32)
                 + jnp.dot(dc_lo, eexp_ref[...], preferred_element_type=F32))
        u = dt_e * xs
        for k in range(SSD_INNER // LANES):
            rows = slice(k * LANES, (k + 1) * LANES)
            uT_ref[rows, :] = u[:, rows].T.astype(BF16)
            d_hi, d_lo = _split2(dec_e[:, rows].T)
            decT_ref[rows, 0:db] = d_hi
            decT_ref[rows, db:2 * db] = d_lo
        for g in range(SSD_GROUPS):
            bm_ref[g] = xc[:, SSD_INNER + g * SSD_STATE:SSD_INNER + (g + 1) * SSD_STATE]
            cT_ref[g] = xc[:, SSD_INNER + gn + g * SSD_STATE:SSD_INNER + gn + (g + 1) * SSD_STATE].T
        yT_ref[...] = jnp.zeros_like(yT_ref)

    row_id = lax.broadcasted_iota(jnp.int32, (db, LANES), 0)
    col_id = lax.broadcasted_iota(jnp.int32, (SSD_STATE, db), 1)
    for j in range(bb):
        b = i * bb + j
        on_row = row_id == b
        sel = jnp.where(on_row, 1.0, 0.0).astype(BF16)
        sel2 = jnp.concatenate([sel, sel], axis=0)
        for g in range(SSD_GROUPS):
            rows = slice(g * half, (g + 1) * half)
            rhs_b = jnp.where(on_row, bm_ref[g], 0.0).astype(BF16)
            upd = jnp.dot(uT_ref[rows, :], rhs_b, preferred_element_type=F32)
            dec_rep = jnp.dot(decT_ref[rows, :], sel2, preferred_element_type=F32)
            hn = dec_rep * h_ref[j, rows, :] + upd
            h_out_ref[j, rows, :] = hn
            rhs_c = jnp.where(col_id == b, cT_ref[g], 0.0).astype(BF16)
            yT_ref[rows, :] += jnp.dot(hn.astype(BF16), rhs_c, preferred_element_type=F32)

    @pl.when(i == pl.num_programs(0) - 1)
    def _():
        ss = jnp.zeros((db, 1), F32)
        gated = []
        for k in range(SSD_INNER // LANES):
            cols = slice(k * LANES, (k + 1) * LANES)
            y = yT_ref[cols, :].T + dskip_ref[:, cols] * xs_ref[:, cols]
            zf = z_ref[:, cols].astype(F32)
            gk = y * (zf * _sigmoid(zf))
            gated.append(gk)
            ss = ss + jnp.sum(gk * gk, axis=-1, keepdims=True)
        scale = lax.rsqrt(ss * (1.0 / SSD_INNER) + NORM_EPS)
        for k in range(SSD_INNER // LANES):
            cols = slice(k * LANES, (k + 1) * LANES)
            y_ref[:, cols] = (gated[k] * scale * nw_ref[:, cols]).astype(BF16)


def _sample_ssd(xbc_s, z_s, dt_s, state_conv, state_ssm, conv_w, conv_b, dt_bias128, a_log128,
                dskip_row, ssd_nw):
    db = xbc_s.shape[0]
    assert db == LANES, "sample SSD kernel keeps the sequences on the lane axis"
    bb = SUBLANES
    _, _, eexp = _ssd_consts()
    k = jnp.arange(LANES)
    eexp1 = (jnp.where(k < SSD_HEADS, k, -1)[:, None] == (jnp.arange(SSD_INNER) // SSD_HEADDIM)[None, :]).astype(BF16)
    del eexp
    sconv2 = state_conv.reshape(db, (SSD_CONV - 1) * SSD_CONV_DIM)
    h3 = state_ssm.reshape(db, SSD_INNER, SSD_STATE)
    full = lambda a: _const_spec(a.shape)
    hblk = pl.BlockSpec((bb, SSD_INNER, SSD_STATE), lambda i: (i, 0, 0))
    y, sconv_new, h_new = pl.pallas_call(
        _sample_ssd_kernel,
        grid=(db // bb,),
        in_specs=[full(xbc_s), full(z_s), full(dt_s), full(sconv2), hblk,
                  full(conv_w), full(conv_b), full(dt_bias128), full(a_log128), full(dskip_row),
                  full(ssd_nw), full(eexp1)],
        out_specs=[pl.BlockSpec((db, SSD_INNER), lambda i: (0, 0)),
                   pl.BlockSpec(sconv2.shape, lambda i: (0, 0)), hblk],
        out_shape=[jax.ShapeDtypeStruct((db, SSD_INNER), BF16),
                   jax.ShapeDtypeStruct(sconv2.shape, F32),
                   jax.ShapeDtypeStruct(h3.shape, F32)],
        scratch_shapes=[pltpu.VMEM((SSD_INNER, db), BF16),
                        pltpu.VMEM((SSD_INNER, 2 * db), BF16),
                        pltpu.VMEM((SSD_GROUPS, db, SSD_STATE), F32),
                        pltpu.VMEM((SSD_GROUPS, SSD_STATE, db), F32),
                        pltpu.VMEM((db, SSD_INNER), F32),
                        pltpu.VMEM((SSD_INNER, db), F32)],
        compiler_params=_cparams(("arbitrary",)),
        name="sample_ssd",
    )(xbc_s, z_s, dt_s, sconv2, h3, conv_w, conv_b, dt_bias128, a_log128, dskip_row, ssd_nw, eexp1)
    return y, sconv_new, h_new


def _cache_copies(c_refs, n_refs, o_refs, sems):
    copies = []
    for g in range(N_PATTERNS):
        wb = c_refs[g].shape[1]
        copies.append(pltpu.make_async_copy(c_refs[g].at[:, pl.ds(1, wb - 1)],
                                            o_refs[g].at[:, pl.ds(0, wb - 1)], sems.at[g]))
        copies.append(pltpu.make_async_copy(n_refs[g], o_refs[g].at[:, wb - 1], sems.at[N_PATTERNS + g]))
    return copies


def _cache_shift_kernel(c0_ref, c1_ref, c2_ref, n0_ref, n1_ref, n2_ref, o0_ref, o1_ref, o2_ref, sems):
    copies = _cache_copies((c0_ref, c1_ref, c2_ref), (n0_ref, n1_ref, n2_ref), (o0_ref, o1_ref, o2_ref), sems)
    for cp in copies:
        cp.start()
    for cp in copies:
        cp.wait()


def _cache_shift(qkv_s, caches):
    db = qkv_s.shape[0]
    q4 = qkv_s.astype(F32).reshape(db, 3 * N_PATTERNS, ATTN_HG, HEAD_DIM)
    new_rows = [q4[:, 3 * g + 1:3 * g + 3] for g in range(N_PATTERNS)]
    any_spec = pl.BlockSpec(memory_space=pl.ANY)
    outs = pl.pallas_call(
        _cache_shift_kernel,
        in_specs=[any_spec] * 3 + [pl.BlockSpec(memory_space=pltpu.VMEM)] * 3,
        out_specs=[any_spec] * 3,
        out_shape=[jax.ShapeDtypeStruct(c.shape, F32) for c in caches],
        scratch_shapes=[pltpu.SemaphoreType.DMA((2 * N_PATTERNS,))],
        compiler_params=pltpu.CompilerParams(vmem_limit_bytes=VMEM_LIMIT),
        name="cache_shift",
    )(*caches, *new_rows)
    return [o[None] for o in outs]


def _outproj_kernel(o0_ref, o1_ref, o2_ref, l0_ref, l1_ref, l2_ref, ssm_ref, x_ref,
                    e3_ref, wa_ref, ws_ref, nw_ref, y_ref):
    tm = x_ref.shape[0]
    hg = ATTN_HG
    lse = l0_ref[...] + l1_ref[...] + l2_ref[...]
    lane = lax.broadcasted_iota(jnp.int32, (tm, LANES), 1)
    used = lane < N_PATTERNS * hg

    def over_groups(v, op):
        r = op(op(v, pltpu.roll(v, LANES - hg, 1)), pltpu.roll(v, LANES - 2 * hg, 1))
        return jnp.where(lane < hg, r, jnp.where(lane < 2 * hg, pltpu.roll(r, hg, 1), pltpu.roll(r, 2 * hg, 1)))

    mx = over_groups(lse, jnp.maximum)
    e = jnp.where(used, jnp.exp(lse - mx), 0.0)
    alpha = jnp.where(used, e / over_groups(e, jnp.add), 0.0)
    a_hi, a_lo = _split2(alpha)
    packed = (a_hi.astype(F32) + pltpu.roll(a_lo.astype(F32), 4 * hg, 1)).astype(BF16)
    a_exp = jnp.dot(packed, e3_ref[...], preferred_element_type=F32)
    attn = (a_exp[:, :ATTN_OUT] * o0_ref[...].astype(F32)
            + a_exp[:, ATTN_OUT:2 * ATTN_OUT] * o1_ref[...].astype(F32)
            + a_exp[:, 2 * ATTN_OUT:] * o2_ref[...].astype(F32))
    mix = (jnp.dot(attn.astype(BF16), wa_ref[...], preferred_element_type=F32)
           + jnp.dot(ssm_ref[...], ws_ref[...], preferred_element_type=F32))
    y_ref[...] = x_ref[...] + _rms_scale(mix) * nw_ref[...]


def _outproj(outs, lses, ssm, x2d, wa, ws, nw):
    m = x2d.shape[0]
    tm = min(ROW_TILE, m)
    k = jnp.arange(LANES)
    slot = jnp.where(k % (4 * ATTN_HG) < N_PATTERNS * ATTN_HG, k % (4 * ATTN_HG), -1)
    slot = jnp.where(k < 8 * ATTN_HG, slot, -1)
    e3 = (slot[:, None] == (jnp.arange(N_PATTERNS * ATTN_OUT) // HEAD_DIM)[None, :]).astype(BF16)
    row = lambda w: pl.BlockSpec((tm, w), lambda i: (i, 0))
    return pl.pallas_call(
        _outproj_kernel,
        grid=(m // tm,),
        in_specs=[row(ATTN_OUT)] * 3 + [row(LANES)] * 3 + [row(SSD_INNER), row(D_MODEL),
                  _const_spec(e3.shape), _const_spec(wa.shape), _const_spec(ws.shape), _const_spec(nw.shape)],
        out_specs=row(D_MODEL),
        out_shape=jax.ShapeDtypeStruct((m, D_MODEL), F32),
        compiler_params=_cparams(("arbitrary",)),
        name="outproj",
    )(*outs, *lses, ssm, x2d, e3, wa, ws, nw)


def _ffn_kernel(y_ref, nw1_ref, wg_ref, wu_ref, wo_ref, nw2_ref, out_ref, xn_ref, acc_ref):
    y = y_ref[...]
    xn_ref[...] = (_rms_scale(y) * nw1_ref[...]).astype(BF16)
    xn = xn_ref[...]
    for j in range(FFN_HIDDEN // FFN_CHUNK):
        cols = slice(j * FFN_CHUNK, (j + 1) * FFN_CHUNK)
        gate = jnp.dot(xn, wg_ref[:, cols], preferred_element_type=F32)
        up = jnp.dot(xn, wu_ref[:, cols], preferred_element_type=F32)
        h = (gate * _sigmoid(gate) * up).astype(BF16)
        part = jnp.dot(h, wo_ref[cols, :], preferred_element_type=F32)
        if j == 0:
            acc_ref[...] = part
        else:
            acc_ref[...] += part
    out_ref[...] = y + _rms_scale(acc_ref[...]) * nw2_ref[...]


def _ffn(y2d, nw1, wg, wu, wo, nw2):
    m = y2d.shape[0]
    tm = min(ROW_TILE, m)
    row = pl.BlockSpec((tm, D_MODEL), lambda i: (i, 0))
    return pl.pallas_call(
        _ffn_kernel,
        grid=(m // tm,),
        in_specs=[row, _const_spec(nw1.shape), _const_spec(wg.shape), _const_spec(wu.shape),
                  _const_spec(wo.shape), _const_spec(nw2.shape)],
        out_specs=row,
        out_shape=jax.ShapeDtypeStruct((m, D_MODEL), F32),
        scratch_shapes=[pltpu.VMEM((tm, D_MODEL), BF16), pltpu.VMEM((tm, D_MODEL), F32)],
        compiler_params=_cparams(("arbitrary",)),
        name="ffn",
    )(y2d, nw1, wg, wu, wo, nw2)


def _rope_tables(pos):
    half = HEAD_DIM // 2
    inv_freq = ROPE_THETA ** (-jnp.arange(half, dtype=F32) / half)
    ang = pos.astype(F32)[:, None] * inv_freq[None, :]
    cos, sin = jnp.cos(ang), jnp.sin(ang)
    reps = LANES // HEAD_DIM
    return (jnp.tile(jnp.concatenate([cos, cos], axis=1), (1, reps)),
            jnp.tile(jnp.concatenate([-sin, sin], axis=1), (1, reps)))


def _pad_lanes(v):
    return jnp.pad(v.astype(F32), (0, LANES - v.shape[0])).reshape(1, LANES)


def kernel(x_prompt, x_sample, cache_kv_w128, cache_kv_w512, cache_kv_w2048, state_conv, state_ssm,
           norm_mix_pre, norm_mix_post, norm_ffn_pre, norm_ffn_post, w_in, w_out, conv_w, conv_b,
           dt_bias, a_log, d_skip, ssd_norm_w, w_ffn_in, w_ffn_out):
    b, s, _ = x_prompt.shape
    db, ds, _ = x_sample.shape
    assert ds == 1 and norm_mix_pre.shape[0] == 1, "one layer, one sample token"
    assert s % ROW_TILE == 0 and s % (WIN_KEYS * ATTN_PATTERNS[-1][1]) == 0
    past_len = 8192

    w_in0 = w_in[0]
    o1, o2, o3 = ATTN_QKV, ATTN_QKV + SSD_INNER, ATTN_QKV + SSD_INNER + SSD_CONV_DIM
    wqkv = w_in0[:, :o1].astype(BF16)
    wz = w_in0[:, o1:o2].astype(BF16)
    wxbc = w_in0[:, o2:o3].astype(BF16)
    wdt = jnp.pad(w_in0[:, o3:], ((0, 0), (0, LANES - SSD_HEADS))).astype(BF16)
    wa = w_out[0][:ATTN_OUT].astype(BF16)
    ws = w_out[0][ATTN_OUT:].astype(BF16)
    wg = w_ffn_in[0][:, :FFN_HIDDEN].astype(BF16)
    wu = w_ffn_in[0][:, FFN_HIDDEN:].astype(BF16)
    wo = w_ffn_out[0].astype(BF16)
    nw_mix_pre, nw_mix_post = norm_mix_pre.reshape(1, -1), norm_mix_post.reshape(1, -1)
    nw_ffn_pre, nw_ffn_post = norm_ffn_pre.reshape(1, -1), norm_ffn_post.reshape(1, -1)
    ssd_nw = ssd_norm_w.reshape(1, -1)
    cw, cb = conv_w[0], conv_b.reshape(1, -1)
    dtb, alog = _pad_lanes(dt_bias[0]), _pad_lanes(a_log[0])
    dskip_row = jnp.repeat(d_skip[0].astype(F32), SSD_HEADDIM).reshape(1, SSD_INNER)
    caches = (cache_kv_w128[0], cache_kv_w512[0], cache_kv_w2048[0])

    xp = x_prompt.reshape(b * s, D_MODEL)
    cos_p, sin_p = _rope_tables(jnp.arange(s))
    qkv, z, xbc, dt_raw = _inproj(xp, nw_mix_pre, cos_p, sin_p, wqkv, wz, wxbc, wdt)
    outs, lses = zip(*[_prompt_attention(qkv, b, s, g) for g in range(N_PATTERNS)])
    ssm, p_conv, h_t = _prompt_ssd(xbc, z, dt_raw, cw, cb, dtb, alog, dskip_row, ssd_nw, b, s)
    y1 = _outproj(outs, lses, ssm.reshape(b * s, SSD_INNER), xp, wa, ws, nw_mix_post)
    y_prompt = _ffn(y1, nw_ffn_pre, wg, wu, wo, nw_ffn_post).reshape(b, s, D_MODEL)

    qkv_p = qkv.reshape(b, s, ATTN_QKV)
    p_kv = []
    for g, (window, _) in enumerate(ATTN_PATTERNS):
        wlen = min(window, s)
        lo = (3 * g + 1) * ATTN_OUT
        kv = qkv_p[:, s - wlen:, lo:lo + 2 * ATTN_OUT].astype(F32)
        p_kv.append(kv.reshape(1, b, wlen, 2, ATTN_HG, HEAD_DIM))
    p_ssm = h_t.reshape(b, SSD_STATE, SSD_HEADS, SSD_HEADDIM).transpose(0, 2, 3, 1)[None]

    xs2 = x_sample.reshape(db, D_MODEL)
    cos_s, sin_s = _rope_tables(jnp.full((min(ROW_TILE, db),), past_len))
    qkv_s, z_s, xbc_s, dt_s = _inproj(xs2, nw_mix_pre, cos_s, sin_s, wqkv, wz, wxbc, wdt)
    outs_s, lses_s = _sample_attention(qkv_s, caches)
    ssm_s, sconv_new, h_new = _sample_ssd(xbc_s, z_s, dt_s, state_conv[0], state_ssm[0], cw, cb, dtb, alog,
                                          dskip_row, ssd_nw)
    s_kv = _cache_shift(qkv_s, caches)
    y1_s = _outproj(outs_s, lses_s, ssm_s, xs2, wa, ws, nw_mix_post)
    y_sample = _ffn(y1_s, nw_ffn_pre, wg, wu, wo, nw_ffn_post).reshape(db, 1, D_MODEL)

    s_conv = sconv_new.reshape(1, db, SSD_CONV - 1, SSD_CONV_DIM)
    s_ssm = h_new.reshape(1, db, SSD_HEADS, SSD_HEADDIM, SSD_STATE)
    return (y_prompt, y_sample, p_kv[0], p_kv[1], p_kv[2], p_conv[None], p_ssm,
            s_kv[0], s_kv[1], s_kv[2], s_conv, s_ssm)
```

```python
import functools

import jax
import jax.numpy as jnp
from jax import lax
from jax.experimental import pallas as pl
from jax.experimental.pallas import tpu as pltpu

F32 = jnp.float32
BF16 = jnp.bfloat16

D_MODEL = 1024
HEAD_DIM = 64
ATTN_PATTERNS = ((128, 1), (512, 4), (2048, 16))
N_PATTERNS = len(ATTN_PATTERNS)
ATTN_HG = 8
WIN_KEYS = 128
ATTN_OUT = ATTN_HG * HEAD_DIM
ATTN_QKV = N_PATTERNS * 3 * ATTN_OUT
ROPE_THETA = 10000.0
ATTN_SCALE = HEAD_DIM ** -0.5
NEG_INF = -1e30
SSD_INNER = D_MODEL
SSD_HEADDIM = 64
SSD_HEADS = SSD_INNER // SSD_HEADDIM
SSD_GROUPS = 2
SSD_HPG = SSD_HEADS // SSD_GROUPS
SSD_STATE = 128
SSD_CONV = 4
SSD_CONV_DIM = SSD_INNER + 2 * SSD_GROUPS * SSD_STATE
FFN_HIDDEN = 2816
NORM_EPS = 1e-6

LANES = 128
SUBLANES = 8
ROW_TILE = 512
SSD_T = 128
FFN_CHUNK = 256
VMEM_LIMIT = 56 * 1024 * 1024
CACHE_BLOCK_BYTES = 4 * 1024 * 1024


def _cparams(sem):
    return pltpu.CompilerParams(dimension_semantics=sem, vmem_limit_bytes=VMEM_LIMIT)


def _const_spec(shape):
    nd = len(shape)
    return pl.BlockSpec(shape, lambda *_: (0,) * nd, pipeline_mode=pl.Buffered(1))


def _split2(x):
    hi = x.astype(BF16)
    lo = (x - hi.astype(F32)).astype(BF16)
    return hi, lo


def _split3(x):
    hi = x.astype(BF16)
    r = x - hi.astype(F32)
    mid = r.astype(BF16)
    lo = (r - mid.astype(F32)).astype(BF16)
    return hi, mid, lo


def _sigmoid(x):
    return 1.0 / (1.0 + jnp.exp(-x))


def _softplus(x):
    return jnp.maximum(x, 0.0) + jnp.log1p(jnp.exp(-jnp.abs(x)))


def _rms_scale(x):
    return x * lax.rsqrt(jnp.mean(x * x, axis=-1, keepdims=True) + NORM_EPS)


def _inproj_kernel(x_ref, nw_ref, cos_ref, sin_ref, wqkv_ref, wz_ref, wxbc_ref, wdt_ref,
                   qkv0_ref, qkv1_ref, qkv2_ref, z_ref, xbc_ref, dt_ref, xn_ref, perm_ref, *, dils):
    tm = x_ref.shape[1]
    xn_ref[...] = (_rms_scale(x_ref[0]) * nw_ref[...]).astype(BF16)
    xn = xn_ref[...]
    cos = cos_ref[...]
    sin = sin_ref[...]
    lane = lax.broadcasted_iota(jnp.int32, (tm, LANES), 1)
    first_half = (lane % HEAD_DIM) < (HEAD_DIM // 2)
    group_refs = (qkv0_ref, qkv1_ref, qkv2_ref)
    for j in range(ATTN_QKV // ATTN_OUT):
        group, part = divmod(j, 3)
        dil = dils[group]
        out_ref = group_refs[group]
        acc = jnp.dot(xn, wqkv_ref[:, j * ATTN_OUT:(j + 1) * ATTN_OUT], preferred_element_type=F32)
        for c in range(ATTN_OUT // LANES):
            a = acc[:, c * LANES:(c + 1) * LANES]
            if part != 2:
                partner = jnp.where(first_half, pltpu.roll(a, LANES - HEAD_DIM // 2, 1),
                                    pltpu.roll(a, HEAD_DIM // 2, 1))
                a = a * cos + partner * sin
                if part == 0:
                    a = a * ATTN_SCALE
            cols = slice(part * ATTN_OUT + c * LANES, part * ATTN_OUT + (c + 1) * LANES)
            if dil == 1:
                out_ref[0, 0, :, cols] = a.astype(BF16)
            else:
                perm_ref[...] = a
                for r in range(dil):
                    out_ref[0, r, :, cols] = perm_ref[pl.ds(r, tm // dil, stride=dil), :].astype(BF16)
    z_ref[0] = jnp.dot(xn, wz_ref[...], preferred_element_type=F32).astype(BF16)
    xbc_ref[0] = jnp.dot(xn, wxbc_ref[...], preferred_element_type=F32).astype(BF16)
    dt_ref[0] = jnp.dot(xn, wdt_ref[...], preferred_element_type=F32)


def _inproj(x3d, nw, cos_t, sin_t, wqkv, wz, wxbc, wdt, dils):
    b, s, _ = x3d.shape
    tm = min(ROW_TILE, s)
    tpb = s // tm
    assert cos_t.shape[0] == s
    row = lambda w: pl.BlockSpec((1, tm, w), lambda i: (i // tpb, i % tpb, 0))
    tab = pl.BlockSpec((tm, LANES), lambda i: (i % tpb, 0))
    grp = lambda d: pl.BlockSpec((1, d, tm // d, 3 * ATTN_OUT), lambda i: (i // tpb, 0, i % tpb, 0))
    return pl.pallas_call(
        functools.partial(_inproj_kernel, dils=dils),
        grid=(b * tpb,),
        in_specs=[row(D_MODEL), _const_spec((1, D_MODEL)), tab, tab,
                  _const_spec(wqkv.shape), _const_spec(wz.shape), _const_spec(wxbc.shape),
                  _const_spec(wdt.shape)],
        out_specs=[grp(d) for d in dils] + [row(SSD_INNER), row(SSD_CONV_DIM), row(LANES)],
        out_shape=[jax.ShapeDtypeStruct((b, d, s // d, 3 * ATTN_OUT), BF16) for d in dils]
                  + [jax.ShapeDtypeStruct((b, s, SSD_INNER), BF16),
                     jax.ShapeDtypeStruct((b, s, SSD_CONV_DIM), BF16),
                     jax.ShapeDtypeStruct((b, s, LANES), F32)],
        scratch_shapes=[pltpu.VMEM((tm, D_MODEL), BF16), pltpu.VMEM((tm, LANES), F32)],
        compiler_params=_cparams(("arbitrary",)),
        name="inproj",
    )(x3d, nw, cos_t, sin_t, wqkv, wz, wxbc, wdt)


def _attn_kernel(q_ref, kc_ref, vc_ref, kp_ref, vp_ref, o_ref, lse_ref, k_scr, v_scr, *, group):
    cq = q_ref.shape[0]
    w = WIN_KEYS
    n = pl.program_id(2)
    k_scr[0:w] = kp_ref[...]
    k_scr[w:] = kc_ref[...]
    v_scr[0:w] = vp_ref[...]
    v_scr[w:] = vc_ref[...]
    lane = lax.broadcasted_iota(jnp.int32, (w, LANES), 1)
    lo = lane < HEAD_DIM
    qi = lax.broadcasted_iota(jnp.int32, (w, 2 * w), 0)
    ki = lax.broadcasted_iota(jnp.int32, (w, 2 * w), 1)
    band = ((ki < w) & (ki >= qi)) | ((ki >= w) & ((ki - w) <= qi))
    for i in range(cq // w):
        if i == 0:
            kmin = jnp.where(n > 0, 0, w)
            valid = band & (ki >= kmin)
        else:
            valid = band
        lse_tile = jnp.zeros((w, LANES), F32)
        for pair in range(ATTN_OUT // LANES):
            cols = slice(pair * LANES, (pair + 1) * LANES)
            qp = q_ref[i * w:(i + 1) * w, cols]
            kk = k_scr[i * w:(i + 2) * w, cols]
            vv = v_scr[i * w:(i + 2) * w, cols]
            halves = []
            for half in range(2):
                qh = jnp.where(lo if half == 0 else jnp.logical_not(lo), qp, jnp.zeros_like(qp))
                s = lax.dot_general(qh, kk, (((1,), (1,)), ((), ())), preferred_element_type=F32)
                s = jnp.where(valid, s, NEG_INF)
                m = jnp.max(s, axis=-1, keepdims=True)
                p = jnp.exp(s - m)
                den = jnp.sum(p, axis=-1, keepdims=True)
                pv = jnp.dot(p.astype(BF16), vv, preferred_element_type=F32)
                halves.append(pv / den)
                head = 2 * pair + half
                lse_tile = jnp.where(lane == group * ATTN_HG + head, m + jnp.log(den), lse_tile)
            o_ref[i * w:(i + 1) * w, cols] = jnp.where(lo, halves[0], halves[1]).astype(BF16)
        lse_ref[i * w:(i + 1) * w, :] = lse_tile


def _prompt_attention(qkv_g, group):
    b, dil, sub_len, _ = qkv_g.shape
    cq = min(ROW_TILE, sub_len)
    blk_per_chunk = cq // WIN_KEYS
    cur = lambda t: pl.BlockSpec((None, None, cq, ATTN_OUT), lambda bi, r, n: (bi, r, n, t))
    prev = lambda t: pl.BlockSpec(
        (None, None, WIN_KEYS, ATTN_OUT),
        lambda bi, r, n: (bi, r, jnp.maximum(n * blk_per_chunk - 1, 0), t))
    out = lambda wdt: pl.BlockSpec((None, None, cq, wdt), lambda bi, r, n: (bi, r, n, 0))
    return pl.pallas_call(
        functools.partial(_attn_kernel, group=group),
        grid=(b, dil, sub_len // cq),
        in_specs=[cur(0), cur(1), cur(2), prev(1), prev(2)],
        out_specs=[out(ATTN_OUT), out(LANES)],
        out_shape=[jax.ShapeDtypeStruct((b, dil, sub_len, ATTN_OUT), BF16),
                   jax.ShapeDtypeStruct((b, dil, sub_len, LANES), F32)],
        scratch_shapes=[pltpu.VMEM((cq + WIN_KEYS, ATTN_OUT), BF16),
                        pltpu.VMEM((cq + WIN_KEYS, ATTN_OUT), BF16)],
        compiler_params=_cparams(("arbitrary", "arbitrary", "arbitrary")),
        name=f"prompt_attn_g{group}",
    )(qkv_g, qkv_g, qkv_g, qkv_g, qkv_g)


def _pack3_lanes(x):
    hi, mid, lo = _split3(x)
    packed = hi.astype(F32) + pltpu.roll(mid.astype(F32), SSD_HEADS, 1) + pltpu.roll(lo.astype(F32), 2 * SSD_HEADS, 1)
    return packed.astype(BF16)


def _ssd_kernel(xbc_ref, z_ref, dtr_ref, convw_ref, convb_ref, dtb_ref, alog_ref, dskip_ref, nw_ref,
                tri3_ref, rexp_ref, eexp_ref,
                y_ref, pconv_ref, hT_out_ref,
                xpad_ref, hT_ref, csb_ref, ybuf_ref):
    t = SSD_T
    c = pl.program_id(1)
    last = pl.num_programs(1) - 1
    pad = SUBLANES
    taps = SSD_CONV

    @pl.when(c == 0)
    def _():
        xpad_ref[0:pad] = jnp.zeros((pad, SSD_CONV_DIM), F32)
        hT_ref[...] = jnp.zeros_like(hT_ref)

    xpad_ref[pad:pad + t] = xbc_ref[0].astype(F32)
    acc = convb_ref[...]
    for j in range(taps):
        off = pad - (taps - 1) + j
        acc = acc + convw_ref[j:j + 1, :] * xpad_ref[off:off + t]

    @pl.when(c == last)
    def _():
        pconv_ref[0] = xpad_ref[pad + t - (taps - 1):pad + t]

    xpad_ref[0:pad] = xpad_ref[t:t + pad]
    xc = acc * _sigmoid(acc)
    xs = xc[:, :SSD_INNER]
    xs_b = xs.astype(BF16)
    gn = SSD_GROUPS * SSD_STATE

    lane = lax.broadcasted_iota(jnp.int32, (t, LANES), 1)
    dt = jnp.where(lane < SSD_HEADS, _softplus(dtr_ref[0] + dtb_ref[...]), 0.0)
    a = dt * (-jnp.exp(alog_ref[...]))
    a3 = jnp.concatenate(_split3(a), axis=0)
    cs = jnp.dot(tri3_ref[...], a3, preferred_element_type=F32)
    cs_last = cs[t - 1:t, :]
    wl = jnp.exp(cs_last - cs) * dt
    csT = cs.T
    dtT = dt.T
    wlT = wl.T
    csb_ref[...] = jnp.dot(_pack3_lanes(cs), rexp_ref[...], preferred_element_type=F32)
    cs_last8 = jnp.broadcast_to(cs_last, (SUBLANES, LANES))
    dec = jnp.exp(jnp.dot(_pack3_lanes(cs_last8), eexp_ref[...], preferred_element_type=F32)[0:1, :])

    ti = lax.broadcasted_iota(jnp.int32, (t, t), 0)
    si = lax.broadcasted_iota(jnp.int32, (t, t), 1)
    tri = si <= ti
    lo = lane < SSD_HEADDIM

    g_mat, bT, yoff = [], [], []
    for g in range(SSD_GROUPS):
        bm = xc[:, SSD_INNER + g * SSD_STATE:SSD_INNER + (g + 1) * SSD_STATE]
        cm = xc[:, SSD_INNER + gn + g * SSD_STATE:SSD_INNER + gn + (g + 1) * SSD_STATE].astype(BF16)
        g_mat.append(lax.dot_general(cm, bm.astype(BF16), (((1,), (1,)), ((), ())),
                                     preferred_element_type=F32))
        bT.append(bm.T)
        cols = slice(g * SSD_HPG * SSD_HEADDIM, (g + 1) * SSD_HPG * SSD_HEADDIM)
        yoff.append(jnp.dot(cm, hT_ref[:, cols].astype(BF16), preferred_element_type=F32))

    ss = jnp.zeros((t, 1), F32)
    pairs_per_group = SSD_HPG // 2
    for pair in range(SSD_HEADS // 2):
        g = pair // pairs_per_group
        cols = slice(pair * LANES, (pair + 1) * LANES)
        w_blocks, b_blocks = [], []
        for h in (2 * pair, 2 * pair + 1):
            seg = csb_ref[:, h * LANES:(h + 1) * LANES] - csT[h:h + 1, :]
            seg = jnp.where(tri, seg, NEG_INF)
            w_blocks.append((g_mat[g] * jnp.exp(seg) * dtT[h:h + 1, :]).astype(BF16))
            b_blocks.append((bT[g] * wlT[h:h + 1, :]).astype(BF16))
        lhs = jnp.concatenate([jnp.concatenate(w_blocks, axis=1), jnp.concatenate(b_blocks, axis=1)], axis=0)
        xp = xs_b[:, cols]
        zero = jnp.zeros_like(xp)
        xbd = jnp.concatenate([jnp.where(lo, xp, zero), jnp.where(lo, zero, xp)], axis=0)
        res = jnp.dot(lhs, xbd, preferred_element_type=F32)
        ecs = jnp.exp(jnp.where(lo, csb_ref[:, 2 * pair * LANES:(2 * pair + 1) * LANES],
                                csb_ref[:, (2 * pair + 1) * LANES:(2 * pair + 2) * LANES]))
        gcol = (pair % pairs_per_group) * LANES
        y = res[:t] + yoff[g][:, gcol:gcol + LANES] * ecs + dskip_ref[:, cols] * xs[:, cols]
        hT_ref[:, cols] = hT_ref[:, cols] * dec[:, cols] + res[t:]
        zf = z_ref[0, :, cols].astype(F32)
        gated = y * (zf * _sigmoid(zf))
        ybuf_ref[:, cols] = gated
        ss = ss + jnp.sum(gated * gated, axis=-1, keepdims=True)

    y_ref[0] = (ybuf_ref[...] * lax.rsqrt(ss * (1.0 / SSD_INNER) + NORM_EPS) * nw_ref[...]).astype(BF16)

    @pl.when(c == last)
    def _():
        hT_out_ref[0] = hT_ref[...]


def _ssd_consts():
    t = SSD_T
    tri = (jnp.arange(t)[:, None] >= jnp.arange(t)[None, :]).astype(BF16)
    tri3 = jnp.concatenate([tri, tri, tri], axis=1)
    k = jnp.arange(LANES)
    piece_head = jnp.where(k < 3 * SSD_HEADS, k % SSD_HEADS, -1)
    rexp = (piece_head[:, None] == (jnp.arange(SSD_HEADS * LANES) // LANES)[None, :]).astype(BF16)
    eexp = (piece_head[:, None] == (jnp.arange(SSD_INNER) // SSD_HEADDIM)[None, :]).astype(BF16)
    return tri3, rexp, eexp


def _prompt_ssd(xbc, z, dt_raw, conv_w, conv_b, dt_bias128, a_log128, dskip_row, ssd_nw, b, s):
    t = SSD_T
    tri3, rexp, eexp = _ssd_consts()
    tok = lambda w: pl.BlockSpec((1, t, w), lambda bi, c: (bi, c, 0))
    per_b = lambda *shape: pl.BlockSpec((1,) + shape, lambda bi, c: (bi,) + (0,) * len(shape))
    return pl.pallas_call(
        _ssd_kernel,
        grid=(b, s // t),
        in_specs=[tok(SSD_CONV_DIM), tok(SSD_INNER), tok(LANES),
                  _const_spec(conv_w.shape), _const_spec(conv_b.shape), _const_spec(dt_bias128.shape),
                  _const_spec(a_log128.shape), _const_spec(dskip_row.shape), _const_spec(ssd_nw.shape),
                  _const_spec(tri3.shape), _const_spec(rexp.shape), _const_spec(eexp.shape)],
        out_specs=[tok(SSD_INNER), per_b(SSD_CONV - 1, SSD_CONV_DIM), per_b(SSD_STATE, SSD_INNER)],
        out_shape=[jax.ShapeDtypeStruct((b, s, SSD_INNER), BF16),
                   jax.ShapeDtypeStruct((b, SSD_CONV - 1, SSD_CONV_DIM), F32),
                   jax.ShapeDtypeStruct((b, SSD_STATE, SSD_INNER), F32)],
        scratch_shapes=[pltpu.VMEM((SUBLANES + t, SSD_CONV_DIM), F32),
                        pltpu.VMEM((SSD_STATE, SSD_INNER), F32),
                        pltpu.VMEM((t, SSD_HEADS * LANES), F32),
                        pltpu.VMEM((t, SSD_INNER), F32)],
        compiler_params=_cparams(("arbitrary", "arbitrary")),
        name="prompt_ssd",
    )(xbc.reshape(b, s, SSD_CONV_DIM), z.reshape(b, s, SSD_INNER), dt_raw.reshape(b, s, LANES),
      conv_w, conv_b, dt_bias128, a_log128, dskip_row, ssd_nw, tri3, rexp, eexp)


def _sample_cache_kernel(q_ref, c_ref, co_ref, o_ref, lse_ref, *, dil):
    bb, _, hb, hd, wb = c_ref.shape
    pos = lax.broadcasted_iota(jnp.int32, (1, wb), 1)
    in_window = (pos & (dil - 1)) == 0
    is_last = lax.broadcasted_iota(jnp.int32, (hd, wb), 1) == wb - 1
    for bi in range(bb):
        for h in range(hb):
            q = q_ref[bi, 0, h]
            k_new = q_ref[bi, 1, h]
            v_new = q_ref[bi, 2, h]
            keys = c_ref[bi, 0, h]
            vals = c_ref[bi, 1, h]
            sc = jnp.where(in_window, jnp.sum(keys * q, axis=0, keepdims=True), NEG_INF)
            sc_new = jnp.sum(k_new * q, axis=0, keepdims=True)
            m = jnp.maximum(jnp.max(sc, axis=1, keepdims=True), sc_new)
            p = jnp.exp(sc - m)
            p_new = jnp.exp(sc_new - m)
            den = jnp.sum(p, axis=1, keepdims=True) + p_new
            num = jnp.sum(vals * p, axis=1, keepdims=True) + v_new * p_new
            o_ref[bi, h] = num / den
            lse_ref[bi, h] = jnp.broadcast_to(m + jnp.log(den), (hd, 1))
            co_ref[bi, 0, h] = jnp.where(is_last, k_new, pltpu.roll(keys, wb - 1, 1))
            co_ref[bi, 1, h] = jnp.where(is_last, v_new, pltpu.roll(vals, wb - 1, 1))


def _sample_cache_step(qkv_g, cache, group):
    db = qkv_g.shape[0]
    window, dil = ATTN_PATTERNS[group]
    wb = cache.shape[1]
    assert wb == window and dil & (dil - 1) == 0, "sample path expects a full window of cached rows"
    c_t = jnp.transpose(cache, (0, 2, 3, 4, 1))
    q5 = qkv_g.astype(F32).reshape(db, 3, ATTN_HG, HEAD_DIM, 1)
    row_bytes = 2 * HEAD_DIM * wb * 4
    hb = max(1, min(ATTN_HG, CACHE_BLOCK_BYTES // row_bytes))
    bb = max(1, min(4, CACHE_BLOCK_BYTES // (row_bytes * ATTN_HG)))
    cblk = pl.BlockSpec((bb, 2, hb, HEAD_DIM, wb), lambda i, j: (i, 0, j, 0, 0))
    col = pl.BlockSpec((bb, hb, HEAD_DIM, 1), lambda i, j: (i, j, 0, 0))
    co, o, lse = pl.pallas_call(
        functools.partial(_sample_cache_kernel, dil=dil),
        grid=(db // bb, ATTN_HG // hb),
        in_specs=[pl.BlockSpec((bb, 3, hb, HEAD_DIM, 1), lambda i, j: (i, 0, j, 0, 0)), cblk],
        out_specs=[cblk, col, col],
        out_shape=[jax.ShapeDtypeStruct(c_t.shape, F32),
                   jax.ShapeDtypeStruct((db, ATTN_HG, HEAD_DIM, 1), F32),
                   jax.ShapeDtypeStruct((db, ATTN_HG, HEAD_DIM, 1), F32)],
        compiler_params=_cparams(("arbitrary", "arbitrary")),
        name=f"sample_cache_g{group}",
    )(q5, c_t)
    return jnp.transpose(co, (0, 4, 1, 2, 3))[None], o.reshape(db, ATTN_OUT), lse[:, :, 0, 0]


def _sample_ssd_kernel(xbc_ref, z_ref, dtr_ref, sconv_ref, h_ref,
                       convw_ref, convb_ref, dtb_ref, alog_ref, dskip_ref, nw_ref, eexp_ref,
                       y_ref, sconv_out_ref, h_out_ref,
                       uT_ref, decT_ref, bm_ref, cT_ref, xs_ref, yT_ref):
    db = xbc_ref.shape[0]
    bb = h_ref.shape[0]
    i = pl.program_id(0)
    half = SSD_HPG * SSD_HEADDIM
    gn = SSD_GROUPS * SSD_STATE
    cd = SSD_CONV_DIM

    @pl.when(i == 0)
    def _():
        new = xbc_ref[...].astype(F32)
        acc = convb_ref[...] + convw_ref[SSD_CONV - 1:SSD_CONV, :] * new
        for j in range(SSD_CONV - 1):
            acc = acc + convw_ref[j:j + 1, :] * sconv_ref[:, j * cd:(j + 1) * cd]
        for j in range(1, SSD_CONV - 1):
            sconv_out_ref[:, (j - 1) * cd:j * cd] = sconv_ref[:, j * cd:(j + 1) * cd]
        sconv_out_ref[:, (SSD_CONV - 2) * cd:] = new
        xc = acc * _sigmoid(acc)
        xs = xc[:, :SSD_INNER]
        xs_ref[...] = xs
        lane = lax.broadcasted_iota(jnp.int32, (db, LANES), 1)
        dt = jnp.where(lane < SSD_HEADS, _softplus(dtr_ref[...] + dtb_ref[...]), 0.0)
        dec = jnp.exp(dt * (-jnp.exp(alog_ref[...])))
        dt_hi, dt_lo = _split2(dt)
        dt_e = (jnp.dot(dt_hi, eexp_ref[...], preferred_element_type=F32)
                + jnp.dot(dt_lo, eexp_ref[...], preferred_element_type=F32))
        dc_hi, dc_lo = _split2(dec)
        dec_e = (jnp.dot(dc_hi, eexp_ref[...], preferred_element_type=F32)
                 + jnp.dot(dc_lo, eexp_ref[...], preferred_element_type=F32))
        u = dt_e * xs
        for k in range(SSD_INNER // LANES):
            rows = slice(k * LANES, (k + 1) * LANES)
            uT_ref[rows, :] = u[:, rows].T.astype(BF16)
            d_hi, d_lo = _split2(dec_e[:, rows].T)
            decT_ref[rows, 0:db] = d_hi
            decT_ref[rows, db:2 * db] = d_lo
        for g in range(SSD_GROUPS):
            bm_ref[g] = xc[:, SSD_INNER + g * SSD_STATE:SSD_INNER + (g + 1) * SSD_STATE]
            cT_ref[g] = xc[:, SSD_INNER + gn + g * SSD_STATE:SSD_INNER + gn + (g + 1) * SSD_STATE].T
        yT_ref[...] = jnp.zeros_like(yT_ref)

    row_id = lax.broadcasted_iota(jnp.int32, (db, LANES), 0)
    col_id = lax.broadcasted_iota(jnp.int32, (SSD_STATE, db), 1)
    for j in range(bb):
        b = i * bb + j
        on_row = row_id == b
        sel = jnp.where(on_row, 1.0, 0.0).astype(BF16)
        sel2 = jnp.concatenate([sel, sel], axis=0)
        for g in range(SSD_GROUPS):
            rows = slice(g * half, (g + 1) * half)
            rhs_b = jnp.where(on_row, bm_ref[g], 0.0).astype(BF16)
            upd = jnp.dot(uT_ref[rows, :], rhs_b, preferred_element_type=F32)
            dec_rep = jnp.dot(decT_ref[rows, :], sel2, preferred_element_type=F32)
            hn = dec_rep * h_ref[j, rows, :] + upd
            h_out_ref[j, rows, :] = hn
            rhs_c = jnp.where(col_id == b, cT_ref[g], 0.0).astype(BF16)
            yT_ref[rows, :] += jnp.dot(hn.astype(BF16), rhs_c, preferred_element_type=F32)

    @pl.when(i == pl.num_programs(0) - 1)
    def _():
        ss = jnp.zeros((db, 1), F32)
        gated = []
        for k in range(SSD_INNER // LANES):
            cols = slice(k * LANES, (k + 1) * LANES)
            y = yT_ref[cols, :].T + dskip_ref[:, cols] * xs_ref[:, cols]
            zf = z_ref[:, cols].astype(F32)
            gk = y * (zf * _sigmoid(zf))
            gated.append(gk)
            ss = ss + jnp.sum(gk * gk, axis=-1, keepdims=True)
        scale = lax.rsqrt(ss * (1.0 / SSD_INNER) + NORM_EPS)
        for k in range(SSD_INNER // LANES):
            cols = slice(k * LANES, (k + 1) * LANES)
            y_ref[:, cols] = (gated[k] * scale * nw_ref[:, cols]).astype(BF16)


def _sample_ssd(xbc_s, z_s, dt_s, state_conv, state_ssm, conv_w, conv_b, dt_bias128, a_log128,
                dskip_row, ssd_nw):
    db = xbc_s.shape[0]
    assert db == LANES, "sample SSD kernel keeps the sequences on the lane axis"
    bb = SUBLANES
    _, _, eexp = _ssd_consts()
    k = jnp.arange(LANES)
    eexp1 = (jnp.where(k < SSD_HEADS, k, -1)[:, None] == (jnp.arange(SSD_INNER) // SSD_HEADDIM)[None, :]).astype(BF16)
    del eexp
    sconv2 = state_conv.reshape(db, (SSD_CONV - 1) * SSD_CONV_DIM)
    h3 = state_ssm.reshape(db, SSD_INNER, SSD_STATE)
    full = lambda a: _const_spec(a.shape)
    hblk = pl.BlockSpec((bb, SSD_INNER, SSD_STATE), lambda i: (i, 0, 0))
    y, sconv_new, h_new = pl.pallas_call(
        _sample_ssd_kernel,
        grid=(db // bb,),
        in_specs=[full(xbc_s), full(z_s), full(dt_s), full(sconv2), hblk,
                  full(conv_w), full(conv_b), full(dt_bias128), full(a_log128), full(dskip_row),
                  full(ssd_nw), full(eexp1)],
        out_specs=[pl.BlockSpec((db, SSD_INNER), lambda i: (0, 0)),
                   pl.BlockSpec(sconv2.shape, lambda i: (0, 0)), hblk],
        out_shape=[jax.ShapeDtypeStruct((db, SSD_INNER), BF16),
                   jax.ShapeDtypeStruct(sconv2.shape, F32),
                   jax.ShapeDtypeStruct(h3.shape, F32)],
        scratch_shapes=[pltpu.VMEM((SSD_INNER, db), BF16),
                        pltpu.VMEM((SSD_INNER, 2 * db), BF16),
                        pltpu.VMEM((SSD_GROUPS, db, SSD_STATE), F32),
                        pltpu.VMEM((SSD_GROUPS, SSD_STATE, db), F32),
                        pltpu.VMEM((db, SSD_INNER), F32),
                        pltpu.VMEM((SSD_INNER, db), F32)],
        compiler_params=_cparams(("arbitrary",)),
        name="sample_ssd",
    )(xbc_s, z_s, dt_s, sconv2, h3, conv_w, conv_b, dt_bias128, a_log128, dskip_row, ssd_nw, eexp1)
    return y, sconv_new, h_new


def _outproj_kernel(o0_ref, o1_ref, o2_ref, l0_ref, l1_ref, l2_ref, ssm_ref, x_ref,
                    e3_ref, wa_ref, ws_ref, nw_ref, y_ref, ob1_ref, ob2_ref, lb1_ref, lb2_ref):
    tm = x_ref.shape[1]
    hg = ATTN_HG

    def natural(ref, buf_ref):
        dil = ref.shape[1]
        if dil == 1:
            return ref[0, 0].astype(F32)
        chunks = []
        for c in range(ref.shape[3] // LANES):
            for r in range(dil):
                buf_ref[c, pl.ds(r, tm // dil, stride=dil), :] = ref[0, r, :, c * LANES:(c + 1) * LANES].astype(F32)
            chunks.append(buf_ref[c])
        return chunks[0] if len(chunks) == 1 else jnp.concatenate(chunks, axis=1)

    lse = natural(l0_ref, None) + natural(l1_ref, lb1_ref) + natural(l2_ref, lb2_ref)
    lane = lax.broadcasted_iota(jnp.int32, (tm, LANES), 1)
    used = lane < N_PATTERNS * hg

    def over_groups(v, op):
        r = op(op(v, pltpu.roll(v, LANES - hg, 1)), pltpu.roll(v, LANES - 2 * hg, 1))
        return jnp.where(lane < hg, r, jnp.where(lane < 2 * hg, pltpu.roll(r, hg, 1), pltpu.roll(r, 2 * hg, 1)))

    mx = over_groups(lse, jnp.maximum)
    e = jnp.where(used, jnp.exp(lse - mx), 0.0)
    alpha = jnp.where(used, e / over_groups(e, jnp.add), 0.0)
    a_hi, a_lo = _split2(alpha)
    packed = (a_hi.astype(F32) + pltpu.roll(a_lo.astype(F32), 4 * hg, 1)).astype(BF16)
    a_exp = jnp.dot(packed, e3_ref[...], preferred_element_type=F32)
    attn = (a_exp[:, :ATTN_OUT] * natural(o0_ref, None)
            + a_exp[:, ATTN_OUT:2 * ATTN_OUT] * natural(o1_ref, ob1_ref)
            + a_exp[:, 2 * ATTN_OUT:] * natural(o2_ref, ob2_ref))
    mix = (jnp.dot(attn.astype(BF16), wa_ref[...], preferred_element_type=F32)
           + jnp.dot(ssm_ref[0], ws_ref[...], preferred_element_type=F32))
    y_ref[0] = x_ref[0] + _rms_scale(mix) * nw_ref[...]


def _outproj(outs, lses, ssm, x3d, wa, ws, nw):
    b, s, _ = x3d.shape
    assert outs[0].shape[1] == 1
    tm = min(ROW_TILE, s)
    tpb = s // tm
    k = jnp.arange(LANES)
    slot = jnp.where(k % (4 * ATTN_HG) < N_PATTERNS * ATTN_HG, k % (4 * ATTN_HG), -1)
    slot = jnp.where(k < 8 * ATTN_HG, slot, -1)
    e3 = (slot[:, None] == (jnp.arange(N_PATTERNS * ATTN_OUT) // HEAD_DIM)[None, :]).astype(BF16)
    row = lambda w: pl.BlockSpec((1, tm, w), lambda i: (i // tpb, i % tpb, 0))
    grp = lambda a: pl.BlockSpec((1, a.shape[1], tm // a.shape[1], a.shape[3]), lambda i: (i // tpb, 0, i % tpb, 0))
    return pl.pallas_call(
        _outproj_kernel,
        grid=(b * tpb,),
        in_specs=[grp(a) for a in outs] + [grp(a) for a in lses] + [row(SSD_INNER), row(D_MODEL),
                  _const_spec(e3.shape), _const_spec(wa.shape), _const_spec(ws.shape), _const_spec(nw.shape)],
        out_specs=row(D_MODEL),
        out_shape=jax.ShapeDtypeStruct((b, s, D_MODEL), F32),
        scratch_shapes=[pltpu.VMEM((ATTN_OUT // LANES, tm, LANES), F32), pltpu.VMEM((ATTN_OUT // LANES, tm, LANES), F32),
                        pltpu.VMEM((1, tm, LANES), F32), pltpu.VMEM((1, tm, LANES), F32)],
        compiler_params=_cparams(("arbitrary",)),
        name="outproj",
    )(*outs, *lses, ssm, x3d, e3, wa, ws, nw)


def _ffn_kernel(y_ref, nw1_ref, wg_ref, wu_ref, wo_ref, nw2_ref, out_ref, xn_ref, acc_ref):
    y = y_ref[...]
    xn_ref[...] = (_rms_scale(y) * nw1_ref[...]).astype(BF16)
    xn = xn_ref[...]
    for j in range(FFN_HIDDEN // FFN_CHUNK):
        cols = slice(j * FFN_CHUNK, (j + 1) * FFN_CHUNK)
        gate = jnp.dot(xn, wg_ref[:, cols], preferred_element_type=F32)
        up = jnp.dot(xn, wu_ref[:, cols], preferred_element_type=F32)
        h = (gate * _sigmoid(gate) * up).astype(BF16)
        part = jnp.dot(h, wo_ref[cols, :], preferred_element_type=F32)
        if j == 0:
            acc_ref[...] = part
        else:
            acc_ref[...] += part
    out_ref[...] = y + _rms_scale(acc_ref[...]) * nw2_ref[...]


def _ffn(y2d, nw1, wg, wu, wo, nw2):
    m = y2d.shape[0]
    tm = min(ROW_TILE, m)
    row = pl.BlockSpec((tm, D_MODEL), lambda i: (i, 0))
    return pl.pallas_call(
        _ffn_kernel,
        grid=(m // tm,),
        in_specs=[row, _const_spec(nw1.shape), _const_spec(wg.shape), _const_spec(wu.shape),
                  _const_spec(wo.shape), _const_spec(nw2.shape)],
        out_specs=row,
        out_shape=jax.ShapeDtypeStruct((m, D_MODEL), F32),
        scratch_shapes=[pltpu.VMEM((tm, D_MODEL), BF16), pltpu.VMEM((tm, D_MODEL), F32)],
        compiler_params=_cparams(("arbitrary",)),
        name="ffn",
    )(y2d, nw1, wg, wu, wo, nw2)


def _rope_tables(pos):
    half = HEAD_DIM // 2
    inv_freq = ROPE_THETA ** (-jnp.arange(half, dtype=F32) / half)
    ang = pos.astype(F32)[:, None] * inv_freq[None, :]
    cos, sin = jnp.cos(ang), jnp.sin(ang)
    reps = LANES // HEAD_DIM
    return (jnp.tile(jnp.concatenate([cos, cos], axis=1), (1, reps)),
            jnp.tile(jnp.concatenate([-sin, sin], axis=1), (1, reps)))


def _pad_lanes(v):
    return jnp.pad(v.astype(F32), (0, LANES - v.shape[0])).reshape(1, LANES)


def kernel(x_prompt, x_sample, cache_kv_w128, cache_kv_w512, cache_kv_w2048, state_conv, state_ssm,
           norm_mix_pre, norm_mix_post, norm_ffn_pre, norm_ffn_post, w_in, w_out, conv_w, conv_b,
           dt_bias, a_log, d_skip, ssd_norm_w, w_ffn_in, w_ffn_out):
    b, s, _ = x_prompt.shape
    db, ds, _ = x_sample.shape
    assert ds == 1 and norm_mix_pre.shape[0] == 1, "one layer, one sample token"
    assert s % ROW_TILE == 0 and s % (WIN_KEYS * ATTN_PATTERNS[-1][1]) == 0
    past_len = 8192

    w_in0 = w_in[0]
    o1, o2, o3 = ATTN_QKV, ATTN_QKV + SSD_INNER, ATTN_QKV + SSD_INNER + SSD_CONV_DIM
    wqkv = w_in0[:, :o1].astype(BF16)
    wz = w_in0[:, o1:o2].astype(BF16)
    wxbc = w_in0[:, o2:o3].astype(BF16)
    wdt = jnp.pad(w_in0[:, o3:], ((0, 0), (0, LANES - SSD_HEADS))).astype(BF16)
    wa = w_out[0][:ATTN_OUT].astype(BF16)
    ws = w_out[0][ATTN_OUT:].astype(BF16)
    wg = w_ffn_in[0][:, :FFN_HIDDEN].astype(BF16)
    wu = w_ffn_in[0][:, FFN_HIDDEN:].astype(BF16)
    wo = w_ffn_out[0].astype(BF16)
    nw_mix_pre, nw_mix_post = norm_mix_pre.reshape(1, -1), norm_mix_post.reshape(1, -1)
    nw_ffn_pre, nw_ffn_post = norm_ffn_pre.reshape(1, -1), norm_ffn_post.reshape(1, -1)
    ssd_nw = ssd_norm_w.reshape(1, -1)
    cw, cb = conv_w[0], conv_b.reshape(1, -1)
    dtb, alog = _pad_lanes(dt_bias[0]), _pad_lanes(a_log[0])
    dskip_row = jnp.repeat(d_skip[0].astype(F32), SSD_HEADDIM).reshape(1, SSD_INNER)
    caches = (cache_kv_w128[0], cache_kv_w512[0], cache_kv_w2048[0])

    dils = tuple(d for _, d in ATTN_PATTERNS)
    cos_p, sin_p = _rope_tables(jnp.arange(s))
    *qkv_groups, z, xbc, dt_raw = _inproj(x_prompt, nw_mix_pre, cos_p, sin_p, wqkv, wz, wxbc, wdt, dils)
    outs, lses = zip(*[_prompt_attention(qkv_groups[g], g) for g in range(N_PATTERNS)])
    ssm, p_conv, h_t = _prompt_ssd(xbc, z, dt_raw, cw, cb, dtb, alog, dskip_row, ssd_nw, b, s)
    y1 = _outproj(outs, lses, ssm, x_prompt, wa, ws, nw_mix_post)
    y_prompt = _ffn(y1.reshape(b * s, D_MODEL), nw_ffn_pre, wg, wu, wo, nw_ffn_post).reshape(b, s, D_MODEL)

    p_kv = []
    for g, (window, dil) in enumerate(ATTN_PATTERNS):
        wlen = min(window, s)
        assert wlen % dil == 0
        kv = qkv_groups[g][:, :, (s - wlen) // dil:, ATTN_OUT:].astype(F32)
        kv = jnp.transpose(kv, (0, 2, 1, 3))
        p_kv.append(kv.reshape(1, b, wlen, 2, ATTN_HG, HEAD_DIM))
    p_ssm = h_t.reshape(b, SSD_STATE, SSD_HEADS, SSD_HEADDIM).transpose(0, 2, 3, 1)[None]

    xs3 = x_sample.reshape(1, db, D_MODEL)
    cos_s, sin_s = _rope_tables(jnp.full((db,), past_len))
    *qkv_s, z_s, xbc_s, dt_s = _inproj(xs3, nw_mix_pre, cos_s, sin_s, wqkv, wz, wxbc, wdt, (1,) * N_PATTERNS)
    s_kv, outs_s, lses_s = [], [], []
    for g in range(N_PATTERNS):
        kv_new, o_g, lse_g = _sample_cache_step(qkv_s[g].reshape(db, 3 * ATTN_OUT), caches[g], g)
        s_kv.append(kv_new)
        outs_s.append(o_g.astype(BF16).reshape(1, 1, db, ATTN_OUT))
        lse_g = jnp.pad(lse_g, ((0, 0), (g * ATTN_HG, LANES - (g + 1) * ATTN_HG)))
        lses_s.append(lse_g.reshape(1, 1, db, LANES))
    ssm_s, sconv_new, h_new = _sample_ssd(xbc_s[0], z_s[0], dt_s[0], state_conv[0], state_ssm[0], cw, cb, dtb, alog,
                                          dskip_row, ssd_nw)
    y1_s = _outproj(outs_s, lses_s, ssm_s[None], xs3, wa, ws, nw_mix_post)
    y_sample = _ffn(y1_s[0], nw_ffn_pre, wg, wu, wo, nw_ffn_post).reshape(db, 1, D_MODEL)

    s_conv = sconv_new.reshape(1, db, SSD_CONV - 1, SSD_CONV_DIM)
    s_ssm = h_new.reshape(1, db, SSD_HEADS, SSD_HEADDIM, SSD_STATE)
    return (y_prompt, y_sample, p_kv[0], p_kv[1], p_kv[2], p_conv[None], p_ssm,
            s_kv[0], s_kv[1], s_kv[2], s_conv, s_ssm)
```

```python
import functools

import jax
import jax.numpy as jnp
from jax import lax
from jax.experimental import pallas as pl
from jax.experimental.pallas import tpu as pltpu

F32 = jnp.float32
BF16 = jnp.bfloat16

D_MODEL = 1024
HEAD_DIM = 64
ATTN_PATTERNS = ((128, 1), (512, 4), (2048, 16))
N_PATTERNS = len(ATTN_PATTERNS)
ATTN_HG = 8
WIN_KEYS = 128
ATTN_OUT = ATTN_HG * HEAD_DIM
ATTN_QKV = N_PATTERNS * 3 * ATTN_OUT
ROPE_THETA = 10000.0
ATTN_SCALE = HEAD_DIM ** -0.5
NEG_INF = -1e30
SSD_INNER = D_MODEL
SSD_HEADDIM = 64
SSD_HEADS = SSD_INNER // SSD_HEADDIM
SSD_GROUPS = 2
SSD_HPG = SSD_HEADS // SSD_GROUPS
SSD_STATE = 128
SSD_CONV = 4
SSD_CONV_DIM = SSD_INNER + 2 * SSD_GROUPS * SSD_STATE
FFN_HIDDEN = 2816
NORM_EPS = 1e-6

LANES = 128
SUBLANES = 8
ROW_TILE = 512
SSD_T = 128
FFN_CHUNK = 256
VMEM_LIMIT = 56 * 1024 * 1024
CACHE_BLOCK_BYTES = 4 * 1024 * 1024


def _cparams(sem):
    return pltpu.CompilerParams(dimension_semantics=sem, vmem_limit_bytes=VMEM_LIMIT)


def _const_spec(shape):
    nd = len(shape)
    return pl.BlockSpec(shape, lambda *_: (0,) * nd, pipeline_mode=pl.Buffered(1))


def _split2(x):
    hi = x.astype(BF16)
    lo = (x - hi.astype(F32)).astype(BF16)
    return hi, lo


def _split3(x):
    hi = x.astype(BF16)
    r = x - hi.astype(F32)
    mid = r.astype(BF16)
    lo = (r - mid.astype(F32)).astype(BF16)
    return hi, mid, lo


def _sigmoid(x):
    return 1.0 / (1.0 + jnp.exp(-x))


def _softplus(x):
    return jnp.maximum(x, 0.0) + jnp.log(1.0 + jnp.exp(-jnp.abs(x)))


def _rms_scale(x):
    return x * lax.rsqrt(jnp.mean(x * x, axis=-1, keepdims=True) + NORM_EPS)


def _with_cache_job(kernel_fn, n_in, n_out, n_scratch, n_cache_in, dil, head_lo):
    def wrapped(*refs):
        ins, rest = refs[:n_in], refs[n_in:]
        c_ins, rest = rest[:n_cache_in], rest[n_cache_in:]
        outs, rest = rest[:n_out], rest[n_out:]
        c_outs, rest = rest[:2], rest[2:]
        scratch, col_ref = rest[:n_scratch], rest[n_scratch]
        _sample_cache_body(c_ins[0], c_ins[1], c_outs[0], c_outs[1], col_ref, head_lo, dil, 1)
        kernel_fn(*ins, *outs, *scratch)
    return wrapped


def _row_tile_call(kernel_fn, job, *, grid, in_specs, inputs, out_specs, out_shape, scratch_shapes, name):
    if job is None:
        res = pl.pallas_call(kernel_fn, grid=grid, in_specs=in_specs, out_specs=out_specs, out_shape=out_shape,
                             scratch_shapes=scratch_shapes, compiler_params=_cparams(("arbitrary",)),
                             name=name)(*inputs)
        return res, None
    q4, cache_t, partial, dil, head_lo, n_heads = job
    db, _, _, hd, wb = cache_t.shape
    assert grid == (db,) and head_lo % n_heads == 0
    cblk = pl.BlockSpec((1, 2, n_heads, hd, wb), lambda i: (i, 0, head_lo // n_heads, 0, 0))
    c_specs = [pl.BlockSpec((1, 3, ATTN_HG, LANES), lambda i: (i, 0, 0, 0)), cblk]
    c_inputs = [q4, cache_t]
    aliases = {}
    if partial is not None:
        c_specs.append(pl.BlockSpec(memory_space=pl.ANY))
        c_inputs.append(partial)
        aliases = {len(in_specs) + 2: len(out_specs)}
    res = pl.pallas_call(
        _with_cache_job(kernel_fn, len(in_specs), len(out_specs), len(scratch_shapes), len(c_specs), dil, head_lo),
        grid=grid,
        in_specs=list(in_specs) + c_specs,
        out_specs=list(out_specs) + [cblk, pl.BlockSpec((1, 1, SUBLANES, LANES), lambda i: (i, 0, 0, 0))],
        out_shape=list(out_shape) + [jax.ShapeDtypeStruct(cache_t.shape, F32),
                                     jax.ShapeDtypeStruct((db, 1, SUBLANES, LANES), F32)],
        scratch_shapes=list(scratch_shapes) + [pltpu.VMEM((3, ATTN_HG, hd, 1), F32)],
        input_output_aliases=aliases,
        compiler_params=_cparams(("arbitrary",)),
        name=name,
    )(*inputs, *c_inputs)
    return res[:len(out_specs)], (res[-2], res[-1])


def _inproj_kernel(x_ref, nw_ref, cos_ref, sin_ref, wqkv_ref, wz_ref, wxbc_ref, wdt_ref,
                   qkv0_ref, qkv1_ref, qkv2_ref, z_ref, xbc_ref, dt_ref, xn_ref, perm_ref, *, dils):
    tm = x_ref.shape[1]
    xn_ref[...] = (_rms_scale(x_ref[0]) * nw_ref[...]).astype(BF16)
    xn = xn_ref[...]
    cos = cos_ref[...]
    sin = sin_ref[...]
    lane = lax.broadcasted_iota(jnp.int32, (tm, LANES), 1)
    first_half = (lane % HEAD_DIM) < (HEAD_DIM // 2)
    group_refs = (qkv0_ref, qkv1_ref, qkv2_ref)
    for j in range(ATTN_QKV // ATTN_OUT):
        group, part = divmod(j, 3)
        dil = dils[group]
        out_ref = group_refs[group]
        acc = jnp.dot(xn, wqkv_ref[:, j * ATTN_OUT:(j + 1) * ATTN_OUT], preferred_element_type=F32)
        for c in range(ATTN_OUT // LANES):
            a = acc[:, c * LANES:(c + 1) * LANES]
            if part != 2:
                partner = jnp.where(first_half, pltpu.roll(a, LANES - HEAD_DIM // 2, 1),
                                    pltpu.roll(a, HEAD_DIM // 2, 1))
                a = a * cos + partner * sin
                if part == 0:
                    a = a * ATTN_SCALE
            cols = slice(part * ATTN_OUT + c * LANES, part * ATTN_OUT + (c + 1) * LANES)
            if dil == 1:
                out_ref[0, 0, :, cols] = a.astype(BF16)
            else:
                perm_ref[...] = a
                for r in range(dil):
                    out_ref[0, r, :, cols] = perm_ref[pl.ds(r, tm // dil, stride=dil), :].astype(BF16)
    z_ref[0] = jnp.dot(xn, wz_ref[...], preferred_element_type=F32).astype(BF16)
    xbc_ref[0] = jnp.dot(xn, wxbc_ref[...], preferred_element_type=F32).astype(BF16)
    dt_ref[0] = jnp.dot(xn, wdt_ref[...], preferred_element_type=F32)


def _inproj(x3d, nw, cos_t, sin_t, wqkv, wz, wxbc, wdt, dils, job=None):
    b, s, _ = x3d.shape
    tm = min(ROW_TILE, s)
    tpb = s // tm
    assert cos_t.shape[0] == s
    row = lambda w: pl.BlockSpec((1, tm, w), lambda i: (i // tpb, i % tpb, 0))
    tab = pl.BlockSpec((tm, LANES), lambda i: (i % tpb, 0))
    grp = lambda d: pl.BlockSpec((1, d, tm // d, 3 * ATTN_OUT), lambda i: (i // tpb, 0, i % tpb, 0))
    return _row_tile_call(
        functools.partial(_inproj_kernel, dils=dils), job,
        grid=(b * tpb,),
        in_specs=[row(D_MODEL), _const_spec((1, D_MODEL)), tab, tab,
                  _const_spec(wqkv.shape), _const_spec(wz.shape), _const_spec(wxbc.shape),
                  _const_spec(wdt.shape)],
        inputs=(x3d, nw, cos_t, sin_t, wqkv, wz, wxbc, wdt),
        out_specs=[grp(d) for d in dils] + [row(SSD_INNER), row(SSD_CONV_DIM), row(LANES)],
        out_shape=[jax.ShapeDtypeStruct((b, d, s // d, 3 * ATTN_OUT), BF16) for d in dils]
                  + [jax.ShapeDtypeStruct((b, s, SSD_INNER), BF16),
                     jax.ShapeDtypeStruct((b, s, SSD_CONV_DIM), BF16),
                     jax.ShapeDtypeStruct((b, s, LANES), F32)],
        scratch_shapes=[pltpu.VMEM((tm, D_MODEL), BF16), pltpu.VMEM((tm, LANES), F32)],
        name="inproj")


def _attn_kernel(q_ref, kc_ref, vc_ref, kp_ref, vp_ref, o_ref, lse_ref, k_scr, v_scr, *, group):
    cq = q_ref.shape[0]
    w = WIN_KEYS
    n = pl.program_id(2)
    k_scr[0:w] = kp_ref[...]
    k_scr[w:] = kc_ref[...]
    v_scr[0:w] = vp_ref[...]
    v_scr[w:] = vc_ref[...]
    lane = lax.broadcasted_iota(jnp.int32, (w, LANES), 1)
    lo = lane < HEAD_DIM
    qi = lax.broadcasted_iota(jnp.int32, (w, 2 * w), 0)
    ki = lax.broadcasted_iota(jnp.int32, (w, 2 * w), 1)
    band = ((ki < w) & (ki >= qi)) | ((ki >= w) & ((ki - w) <= qi))
    for i in range(cq // w):
        if i == 0:
            kmin = jnp.where(n > 0, 0, w)
            valid = band & (ki >= kmin)
        else:
            valid = band
        lse_tile = jnp.zeros((w, LANES), F32)
        for pair in range(ATTN_OUT // LANES):
            cols = slice(pair * LANES, (pair + 1) * LANES)
            qp = q_ref[i * w:(i + 1) * w, cols]
            kk = k_scr[i * w:(i + 2) * w, cols]
            vv = v_scr[i * w:(i + 2) * w, cols]
            halves = []
            for half in range(2):
                qh = jnp.where(lo if half == 0 else jnp.logical_not(lo), qp, jnp.zeros_like(qp))
                s = lax.dot_general(qh, kk, (((1,), (1,)), ((), ())), preferred_element_type=F32)
                s = jnp.where(valid, s, NEG_INF)
                m = jnp.max(s, axis=-1, keepdims=True)
                p = jnp.exp(s - m)
                den = jnp.sum(p, axis=-1, keepdims=True)
                pv = jnp.dot(p.astype(BF16), vv, preferred_element_type=F32)
                halves.append(pv / den)
                head = 2 * pair + half
                lse_tile = jnp.where(lane == group * ATTN_HG + head, m + jnp.log(den), lse_tile)
            o_ref[i * w:(i + 1) * w, cols] = jnp.where(lo, halves[0], halves[1]).astype(BF16)
        lse_ref[i * w:(i + 1) * w, :] = lse_tile


def _prompt_attention(qkv_g, group):
    b, dil, sub_len, _ = qkv_g.shape
    cq = min(ROW_TILE, sub_len)
    blk_per_chunk = cq // WIN_KEYS
    cur = lambda t: pl.BlockSpec((None, None, cq, ATTN_OUT), lambda bi, r, n: (bi, r, n, t))
    prev = lambda t: pl.BlockSpec(
        (None, None, WIN_KEYS, ATTN_OUT),
        lambda bi, r, n: (bi, r, jnp.maximum(n * blk_per_chunk - 1, 0), t))
    out = lambda wdt: pl.BlockSpec((None, None, cq, wdt), lambda bi, r, n: (bi, r, n, 0))
    return pl.pallas_call(
        functools.partial(_attn_kernel, group=group),
        grid=(b, dil, sub_len // cq),
        in_specs=[cur(0), cur(1), cur(2), prev(1), prev(2)],
        out_specs=[out(ATTN_OUT), out(LANES)],
        out_shape=[jax.ShapeDtypeStruct((b, dil, sub_len, ATTN_OUT), BF16),
                   jax.ShapeDtypeStruct((b, dil, sub_len, LANES), F32)],
        scratch_shapes=[pltpu.VMEM((cq + WIN_KEYS, ATTN_OUT), BF16),
                        pltpu.VMEM((cq + WIN_KEYS, ATTN_OUT), BF16)],
        compiler_params=_cparams(("arbitrary", "arbitrary", "arbitrary")),
        name=f"prompt_attn_g{group}",
    )(qkv_g, qkv_g, qkv_g, qkv_g, qkv_g)


def _pack3_lanes(x):
    hi, mid, lo = _split3(x)
    packed = hi.astype(F32) + pltpu.roll(mid.astype(F32), SSD_HEADS, 1) + pltpu.roll(lo.astype(F32), 2 * SSD_HEADS, 1)
    return packed.astype(BF16)


def _ssd_kernel(xbc_ref, z_ref, dtr_ref, convw_ref, convb_ref, dtb_ref, alog_ref, dskip_ref, nw_ref,
                tri3_ref, rexp_ref, eexp_ref, shift_ref,
                y_ref, pconv_ref, hT_out_ref,
                x2_ref, hT_ref, csb_ref, ybuf_ref):
    t = SSD_T
    c = pl.program_id(1)
    last = pl.num_programs(1) - 1
    taps = SSD_CONV

    @pl.when(c == 0)
    def _():
        x2_ref[0:t] = jnp.zeros((t, SSD_CONV_DIM), BF16)
        hT_ref[...] = jnp.zeros_like(hT_ref)

    cur = xbc_ref[0]
    x2_ref[t:] = cur
    cur_f = cur.astype(F32)
    acc = convb_ref[...] + convw_ref[taps - 1:taps, :] * cur_f
    for j in range(taps - 1):
        shifted = jnp.dot(shift_ref[j], x2_ref[...], preferred_element_type=F32)
        acc = acc + convw_ref[j:j + 1, :] * shifted

    @pl.when(c == last)
    def _():
        pconv_ref[0] = cur_f[t - (taps - 1):, :]

    x2_ref[0:t] = cur
    xc = acc * _sigmoid(acc)
    xs = xc[:, :SSD_INNER]
    xs_b = xs.astype(BF16)
    gn = SSD_GROUPS * SSD_STATE

    lane = lax.broadcasted_iota(jnp.int32, (t, LANES), 1)
    dt = jnp.where(lane < SSD_HEADS, _softplus(dtr_ref[0] + dtb_ref[...]), 0.0)
    a = dt * (-jnp.exp(alog_ref[...]))
    a3 = jnp.concatenate(_split3(a), axis=0)
    cs = jnp.dot(tri3_ref[...], a3, preferred_element_type=F32)
    cs_last = cs[t - 1:t, :]
    wl = jnp.exp(cs_last - cs) * dt
    csT = cs.T
    dtT = dt.T
    wlT = wl.T
    csb_ref[...] = jnp.dot(_pack3_lanes(cs), rexp_ref[...], preferred_element_type=F32)
    cs_last8 = jnp.broadcast_to(cs_last, (SUBLANES, LANES))
    dec = jnp.exp(jnp.dot(_pack3_lanes(cs_last8), eexp_ref[...], preferred_element_type=F32)[0:1, :])

    ti = lax.broadcasted_iota(jnp.int32, (t, t), 0)
    si = lax.broadcasted_iota(jnp.int32, (t, t), 1)
    tri = si <= ti
    lo = lane < SSD_HEADDIM

    g_mat, bT, yoff = [], [], []
    for g in range(SSD_GROUPS):
        bm = xc[:, SSD_INNER + g * SSD_STATE:SSD_INNER + (g + 1) * SSD_STATE]
        cm = xc[:, SSD_INNER + gn + g * SSD_STATE:SSD_INNER + gn + (g + 1) * SSD_STATE].astype(BF16)
        g_mat.append(lax.dot_general(cm, bm.astype(BF16), (((1,), (1,)), ((), ())),
                                     preferred_element_type=F32))
        bT.append(bm.T)
        cols = slice(g * SSD_HPG * SSD_HEADDIM, (g + 1) * SSD_HPG * SSD_HEADDIM)
        yoff.append(jnp.dot(cm, hT_ref[:, cols].astype(BF16), preferred_element_type=F32))

    ss = jnp.zeros((t, 1), F32)
    pairs_per_group = SSD_HPG // 2
    for pair in range(SSD_HEADS // 2):
        g = pair // pairs_per_group
        cols = slice(pair * LANES, (pair + 1) * LANES)
        w_blocks, b_blocks = [], []
        for h in (2 * pair, 2 * pair + 1):
            seg = csb_ref[:, h * LANES:(h + 1) * LANES] - csT[h:h + 1, :]
            seg = jnp.where(tri, seg, NEG_INF)
            w_blocks.append((g_mat[g] * jnp.exp(seg) * dtT[h:h + 1, :]).astype(BF16))
            b_blocks.append((bT[g] * wlT[h:h + 1, :]).astype(BF16))
        lhs = jnp.concatenate([jnp.concatenate(w_blocks, axis=1), jnp.concatenate(b_blocks, axis=1)], axis=0)
        xp = xs_b[:, cols]
        zero = jnp.zeros_like(xp)
        xbd = jnp.concatenate([jnp.where(lo, xp, zero), jnp.where(lo, zero, xp)], axis=0)
        res = jnp.dot(lhs, xbd, preferred_element_type=F32)
        ecs = jnp.exp(jnp.where(lo, csb_ref[:, 2 * pair * LANES:(2 * pair + 1) * LANES],
                                csb_ref[:, (2 * pair + 1) * LANES:(2 * pair + 2) * LANES]))
        gcol = (pair % pairs_per_group) * LANES
        y = res[:t] + yoff[g][:, gcol:gcol + LANES] * ecs + dskip_ref[:, cols] * xs[:, cols]
        hT_ref[:, cols] = hT_ref[:, cols] * dec[:, cols] + res[t:]
        zf = z_ref[0, :, cols].astype(F32)
        gated = y * (zf * _sigmoid(zf))
        ybuf_ref[:, cols] = gated
        ss = ss + jnp.sum(gated * gated, axis=-1, keepdims=True)

    y_ref[0] = (ybuf_ref[...] * lax.rsqrt(ss * (1.0 / SSD_INNER) + NORM_EPS) * nw_ref[...]).astype(BF16)

    @pl.when(c == last)
    def _():
        hT_out_ref[0] = hT_ref[...]


def _ssd_consts():
    t = SSD_T
    tri = (jnp.arange(t)[:, None] >= jnp.arange(t)[None, :]).astype(BF16)
    tri3 = jnp.concatenate([tri, tri, tri], axis=1)
    k = jnp.arange(LANES)
    piece_head = jnp.where(k < 3 * SSD_HEADS, k % SSD_HEADS, -1)
    rexp = (piece_head[:, None] == (jnp.arange(SSD_HEADS * LANES) // LANES)[None, :]).astype(BF16)
    eexp = (piece_head[:, None] == (jnp.arange(SSD_INNER) // SSD_HEADDIM)[None, :]).astype(BF16)
    return tri3, rexp, eexp


def _prompt_ssd(xbc, z, dt_raw, conv_w, conv_b, dt_bias128, a_log128, dskip_row, ssd_nw, b, s):
    t = SSD_T
    tri3, rexp, eexp = _ssd_consts()
    back = (SSD_CONV - 1 - jnp.arange(SSD_CONV - 1))[:, None, None]
    shift = (jnp.arange(2 * t)[None, None, :] == t + jnp.arange(t)[None, :, None] - back).astype(BF16)
    tok = lambda w: pl.BlockSpec((1, t, w), lambda bi, c: (bi, c, 0))
    per_b = lambda *shape: pl.BlockSpec((1,) + shape, lambda bi, c: (bi,) + (0,) * len(shape))
    return pl.pallas_call(
        _ssd_kernel,
        grid=(b, s // t),
        in_specs=[tok(SSD_CONV_DIM), tok(SSD_INNER), tok(LANES),
                  _const_spec(conv_w.shape), _const_spec(conv_b.shape), _const_spec(dt_bias128.shape),
                  _const_spec(a_log128.shape), _const_spec(dskip_row.shape), _const_spec(ssd_nw.shape),
                  _const_spec(tri3.shape), _const_spec(rexp.shape), _const_spec(eexp.shape),
                  _const_spec(shift.shape)],
        out_specs=[tok(SSD_INNER), per_b(SSD_CONV - 1, SSD_CONV_DIM), per_b(SSD_STATE, SSD_INNER)],
        out_shape=[jax.ShapeDtypeStruct((b, s, SSD_INNER), BF16),
                   jax.ShapeDtypeStruct((b, SSD_CONV - 1, SSD_CONV_DIM), F32),
                   jax.ShapeDtypeStruct((b, SSD_STATE, SSD_INNER), F32)],
        scratch_shapes=[pltpu.VMEM((2 * t, SSD_CONV_DIM), BF16),
                        pltpu.VMEM((SSD_STATE, SSD_INNER), F32),
                        pltpu.VMEM((t, SSD_HEADS * LANES), F32),
                        pltpu.VMEM((t, SSD_INNER), F32)],
        compiler_params=_cparams(("arbitrary", "arbitrary")),
        name="prompt_ssd",
    )(xbc.reshape(b, s, SSD_CONV_DIM), z.reshape(b, s, SSD_INNER), dt_raw.reshape(b, s, LANES),
      conv_w, conv_b, dt_bias128, a_log128, dskip_row, ssd_nw, tri3, rexp, eexp, shift)


def _pkv_kernel(k_ref, v_ref, out_ref, buf_ref):
    _, dil, rows, _ = k_ref.shape
    tb = dil * rows
    for part, ref in enumerate((k_ref, v_ref)):
        for c in range(ATTN_OUT // LANES):
            cols = slice(c * LANES, (c + 1) * LANES)
            if dil == 1:
                nat = ref[0, 0, :, cols].astype(F32)
            else:
                for r in range(dil):
                    buf_ref[pl.ds(r, rows, stride=dil), :] = ref[0, r, :, cols].astype(F32)
                nat = buf_ref[...]
            base = part * ATTN_OUT + c * LANES
            for tblk in range(tb // LANES):
                out_ref[0, base:base + LANES, tblk * LANES:(tblk + 1) * LANES] = nat[tblk * LANES:(tblk + 1) * LANES, :].T


def _prompt_kv_tail(qkv_g, window, s):
    b, dil, _, _ = qkv_g.shape
    wlen = min(window, s)
    tb = min(ROW_TILE, wlen)
    assert (s - wlen) % tb == 0 and tb % (dil * 2 * SUBLANES) == 0
    first = (s - wlen) // tb
    blk = lambda t: pl.BlockSpec((1, dil, tb // dil, ATTN_OUT), lambda bi, n: (bi, 0, first + n, t))
    out = pl.pallas_call(
        _pkv_kernel,
        grid=(b, wlen // tb),
        in_specs=[blk(1), blk(2)],
        out_specs=pl.BlockSpec((1, 2 * ATTN_OUT, tb), lambda bi, n: (bi, 0, n)),
        out_shape=jax.ShapeDtypeStruct((b, 2 * ATTN_OUT, wlen), F32),
        scratch_shapes=[pltpu.VMEM((tb, LANES), F32)],
        compiler_params=_cparams(("arbitrary", "arbitrary")),
        name=f"prompt_kv_tail_w{window}",
    )(qkv_g, qkv_g)
    return jnp.transpose(out.reshape(1, b, 2, ATTN_HG, HEAD_DIM, wlen), (0, 1, 5, 2, 3, 4))


def _sample_cache_kernel(q_ref, c_ref, co_ref, o_ref, col_ref, *, dil, hv):
    _sample_cache_body(q_ref, c_ref, co_ref, o_ref, col_ref, pl.program_id(1) * c_ref.shape[2], dil, hv)


def _sample_cache_body(q_ref, c_ref, co_ref, o_ref, col_ref, head0, dil, hv):
    bb, _, hb, hd, wb = c_ref.shape
    pos = lax.broadcasted_iota(jnp.int32, (1, 1, wb), 2)
    in_window = (pos & (dil - 1)) == 0
    is_last = lax.broadcasted_iota(jnp.int32, (1, hd, wb), 2) == wb - 1
    lane = lax.broadcasted_iota(jnp.int32, (hd, LANES), 1)
    lane_row = lax.broadcasted_iota(jnp.int32, (1, LANES), 1)
    for bi in range(bb):
        for part in range(3):
            cols = q_ref[bi, part].T
            for h in range(ATTN_HG):
                col_ref[part, h] = cols[0:hd, h:h + 1]
        o_cols = jnp.zeros((hd, LANES), F32)
        lse_row = jnp.zeros((1, LANES), F32)
        for h in range(0, hb, hv):
            q = col_ref[0, pl.ds(head0 + h, hv)]
            k_new = col_ref[1, pl.ds(head0 + h, hv)]
            v_new = col_ref[2, pl.ds(head0 + h, hv)]
            keys = c_ref[bi, 0, h:h + hv]
            vals = c_ref[bi, 1, h:h + hv]
            sc = jnp.where(in_window, jnp.sum(keys * q, axis=1, keepdims=True), NEG_INF)
            sc_new = jnp.sum(k_new * q, axis=1, keepdims=True)
            m = jnp.maximum(jnp.max(sc, axis=2, keepdims=True), sc_new)
            p = jnp.exp(sc - m)
            p_new = jnp.exp(sc_new - m)
            den = jnp.sum(p, axis=2, keepdims=True) + p_new
            out = (jnp.sum(vals * p, axis=2, keepdims=True) + v_new * p_new) / den
            lse = m + jnp.log(den)
            for t in range(hv):
                o_cols = jnp.where(lane == h + t, out[t], o_cols)
                lse_row = jnp.where(lane_row == h + t, lse[t], lse_row)
            co_ref[bi, 0, h:h + hv] = jnp.where(is_last, k_new, pltpu.roll(keys, wb - 1, 2))
            co_ref[bi, 1, h:h + hv] = jnp.where(is_last, v_new, pltpu.roll(vals, wb - 1, 2))
        tile = jnp.concatenate([o_cols, jnp.broadcast_to(lse_row, (SUBLANES, LANES)),
                                jnp.zeros((LANES - hd - SUBLANES, LANES), F32)], axis=0)
        o_ref[bi, 0] = tile.T[0:SUBLANES, :]


def _sample_cache_step(qkv_g, cache, group):
    db = qkv_g.shape[0]
    window, dil = ATTN_PATTERNS[group]
    wb = cache.shape[1]
    assert wb == window and dil & (dil - 1) == 0, "sample path expects a full window of cached rows"
    c_t, q4 = _sample_cache_inputs(qkv_g, cache)
    row_bytes = 2 * HEAD_DIM * wb * 4
    hb = max(1, min(ATTN_HG, CACHE_BLOCK_BYTES // row_bytes))
    bb = max(1, min(SUBLANES, CACHE_BLOCK_BYTES // (row_bytes * ATTN_HG)))
    hv = hb if hb == ATTN_HG else 1
    n_hblk = ATTN_HG // hb
    cblk = pl.BlockSpec((bb, 2, hb, HEAD_DIM, wb), lambda i, j: (i, 0, j, 0, 0))
    co, o = pl.pallas_call(
        functools.partial(_sample_cache_kernel, dil=dil, hv=hv),
        grid=(db // bb, n_hblk),
        in_specs=[pl.BlockSpec((bb, 3, ATTN_HG, LANES), lambda i, j: (i, 0, 0, 0)), cblk],
        out_specs=[cblk, pl.BlockSpec((bb, 1, SUBLANES, LANES), lambda i, j: (i, j, 0, 0))],
        out_shape=[jax.ShapeDtypeStruct(c_t.shape, F32),
                   jax.ShapeDtypeStruct((db, n_hblk, SUBLANES, LANES), F32)],
        scratch_shapes=[pltpu.VMEM((3, ATTN_HG, HEAD_DIM, 1), F32)],
        compiler_params=_cparams(("arbitrary", "arbitrary")),
        name=f"sample_cache_g{group}",
    )(q4, c_t)
    return _sample_cache_outputs(co, o[:, :, :hb].reshape(db, ATTN_HG, LANES))


def _sample_cache_inputs(qkv_g, cache):
    db = qkv_g.shape[0]
    c_t = jnp.transpose(cache, (0, 2, 3, 4, 1))
    q4 = jnp.pad(qkv_g.astype(F32).reshape(db, 3, ATTN_HG, HEAD_DIM), ((0, 0), (0, 0), (0, 0), (0, LANES - HEAD_DIM)))
    return c_t, q4


def _sample_cache_outputs(co, rows):
    db = rows.shape[0]
    return jnp.transpose(co, (0, 4, 1, 2, 3))[None], rows[:, :, :HEAD_DIM].reshape(db, ATTN_OUT), rows[:, :, HEAD_DIM]


def _sample_ssd_kernel(xbc_ref, z_ref, dtr_ref, sconv_ref, h_ref,
                       convw_ref, convb_ref, dtb_ref, alog_ref, dskip_ref, nw_ref, eexp_ref,
                       y_ref, sconv_out_ref, h_out_ref,
                       uT_ref, decT_ref, bm_ref, cT_ref, xs_ref, yT_ref):
    db = xbc_ref.shape[0]
    bb = h_ref.shape[0]
    i = pl.program_id(0)
    half = SSD_HPG * SSD_HEADDIM
    gn = SSD_GROUPS * SSD_STATE
    cd = SSD_CONV_DIM

    @pl.when(i == 0)
    def _():
        new = xbc_ref[...].astype(F32)
        acc = convb_ref[...] + convw_ref[SSD_CONV - 1:SSD_CONV, :] * new
        for j in range(SSD_CONV - 1):
            acc = acc + convw_ref[j:j + 1, :] * sconv_ref[:, j * cd:(j + 1) * cd]
        for j in range(1, SSD_CONV - 1):
            sconv_out_ref[:, (j - 1) * cd:j * cd] = sconv_ref[:, j * cd:(j + 1) * cd]
        sconv_out_ref[:, (SSD_CONV - 2) * cd:] = new
        xc = acc * _sigmoid(acc)
        xs = xc[:, :SSD_INNER]
        xs_ref[...] = xs
        lane = lax.broadcasted_iota(jnp.int32, (db, LANES), 1)
        dt = jnp.where(lane < SSD_HEADS, _softplus(dtr_ref[...] + dtb_ref[...]), 0.0)
        dec = jnp.exp(dt * (-jnp.exp(alog_ref[...])))
        dt_hi, dt_lo = _split2(dt)
        dt_e = (jnp.dot(dt_hi, eexp_ref[...], preferred_element_type=F32)
                + jnp.dot(dt_lo, eexp_ref[...], preferred_element_type=F32))
        dc_hi, dc_lo = _split2(dec)
        dec_e = (jnp.dot(dc_hi, eexp_ref[...], preferred_element_type=F32)
                 + jnp.dot(dc_lo, eexp_ref[...], preferred_element_type=F32))
        u = dt_e * xs
        for k in range(SSD_INNER // LANES):
            rows = slice(k * LANES, (k + 1) * LANES)
            uT_ref[rows, :] = u[:, rows].T.astype(BF16)
            d_hi, d_lo = _split2(dec_e[:, rows].T)
            decT_ref[rows, 0:db] = d_hi
            decT_ref[rows, db:2 * db] = d_lo
        for g in range(SSD_GROUPS):
            bm_ref[g] = xc[:, SSD_INNER + g * SSD_STATE:SSD_INNER + (g + 1) * SSD_STATE]
            cT_ref[g] = xc[:, SSD_INNER + gn + g * SSD_STATE:SSD_INNER + gn + (g + 1) * SSD_STATE].T
        yT_ref[...] = jnp.zeros_like(yT_ref)

    row_id = lax.broadcasted_iota(jnp.int32, (db, LANES), 0)
    col_id = lax.broadcasted_iota(jnp.int32, (SSD_STATE, db), 1)
    for j in range(bb):
        b = i * bb + j
        on_row = row_id == b
        sel = jnp.where(on_row, 1.0, 0.0).astype(BF16)
        sel2 = jnp.concatenate([sel, sel], axis=0)
        for g in range(SSD_GROUPS):
            rows = slice(g * half, (g + 1) * half)
            rhs_b = jnp.where(on_row, bm_ref[g], 0.0).astype(BF16)
            upd = jnp.dot(uT_ref[rows, :], rhs_b, preferred_element_type=F32)
            dec_rep = jnp.dot(decT_ref[rows, :], sel2, preferred_element_type=F32)
            hn = dec_rep * h_ref[j, rows, :] + upd
            h_out_ref[j, rows, :] = hn
            rhs_c = jnp.where(col_id == b, cT_ref[g], 0.0).astype(BF16)
            yT_ref[rows, :] += jnp.dot(hn.astype(BF16), rhs_c, preferred_element_type=F32)

    @pl.when(i == pl.num_programs(0) - 1)
    def _():
        ss = jnp.zeros((db, 1), F32)
        gated = []
        for k in range(SSD_INNER // LANES):
            cols = slice(k * LANES, (k + 1) * LANES)
            y = yT_ref[cols, :].T + dskip_ref[:, cols] * xs_ref[:, cols]
            zf = z_ref[:, cols].astype(F32)
            gk = y * (zf * _sigmoid(zf))
            gated.append(gk)
            ss = ss + jnp.sum(gk * gk, axis=-1, keepdims=True)
        scale = lax.rsqrt(ss * (1.0 / SSD_INNER) + NORM_EPS)
        for k in range(SSD_INNER // LANES):
            cols = slice(k * LANES, (k + 1) * LANES)
            y_ref[:, cols] = (gated[k] * scale * nw_ref[:, cols]).astype(BF16)


def _sample_ssd(xbc_s, z_s, dt_s, state_conv, state_ssm, conv_w, conv_b, dt_bias128, a_log128,
                dskip_row, ssd_nw):
    db = xbc_s.shape[0]
    assert db == LANES, "sample SSD kernel keeps the sequences on the lane axis"
    bb = SUBLANES
    _, _, eexp = _ssd_consts()
    k = jnp.arange(LANES)
    eexp1 = (jnp.where(k < SSD_HEADS, k, -1)[:, None] == (jnp.arange(SSD_INNER) // SSD_HEADDIM)[None, :]).astype(BF16)
    del eexp
    sconv2 = state_conv.reshape(db, (SSD_CONV - 1) * SSD_CONV_DIM)
    h3 = state_ssm.reshape(db, SSD_INNER, SSD_STATE)
    full = lambda a: _const_spec(a.shape)
    hblk = pl.BlockSpec((bb, SSD_INNER, SSD_STATE), lambda i: (i, 0, 0))
    y, sconv_new, h_new = pl.pallas_call(
        _sample_ssd_kernel,
        grid=(db // bb,),
        in_specs=[full(xbc_s), full(z_s), full(dt_s), full(sconv2), hblk,
                  full(conv_w), full(conv_b), full(dt_bias128), full(a_log128), full(dskip_row),
                  full(ssd_nw), full(eexp1)],
        out_specs=[pl.BlockSpec((db, SSD_INNER), lambda i: (0, 0)),
                   pl.BlockSpec(sconv2.shape, lambda i: (0, 0)), hblk],
        out_shape=[jax.ShapeDtypeStruct((db, SSD_INNER), BF16),
                   jax.ShapeDtypeStruct(sconv2.shape, F32),
                   jax.ShapeDtypeStruct(h3.shape, F32)],
        scratch_shapes=[pltpu.VMEM((SSD_INNER, db), BF16),
                        pltpu.VMEM((SSD_INNER, 2 * db), BF16),
                        pltpu.VMEM((SSD_GROUPS, db, SSD_STATE), F32),
                        pltpu.VMEM((SSD_GROUPS, SSD_STATE, db), F32),
                        pltpu.VMEM((db, SSD_INNER), F32),
                        pltpu.VMEM((SSD_INNER, db), F32)],
        compiler_params=_cparams(("arbitrary",)),
        name="sample_ssd",
    )(xbc_s, z_s, dt_s, sconv2, h3, conv_w, conv_b, dt_bias128, a_log128, dskip_row, ssd_nw, eexp1)
    return y, sconv_new, h_new


def _outproj_kernel(o0_ref, o1_ref, o2_ref, l0_ref, l1_ref, l2_ref, ssm_ref, x_ref,
                    e3_ref, wa_ref, ws_ref, nw_ref, y_ref, ob1_ref, ob2_ref, lb1_ref, lb2_ref):
    tm = x_ref.shape[1]
    hg = ATTN_HG

    def natural(ref, buf_ref):
        dil = ref.shape[1]
        if dil == 1:
            return ref[0, 0].astype(F32)
        chunks = []
        for c in range(ref.shape[3] // LANES):
            for r in range(dil):
                buf_ref[c, pl.ds(r, tm // dil, stride=dil), :] = ref[0, r, :, c * LANES:(c + 1) * LANES].astype(F32)
            chunks.append(buf_ref[c])
        return chunks[0] if len(chunks) == 1 else jnp.concatenate(chunks, axis=1)

    lse = natural(l0_ref, None) + natural(l1_ref, lb1_ref) + natural(l2_ref, lb2_ref)
    lane = lax.broadcasted_iota(jnp.int32, (tm, LANES), 1)
    used = lane < N_PATTERNS * hg

    def over_groups(v, op):
        r = op(op(v, pltpu.roll(v, LANES - hg, 1)), pltpu.roll(v, LANES - 2 * hg, 1))
        return jnp.where(lane < hg, r, jnp.where(lane < 2 * hg, pltpu.roll(r, hg, 1), pltpu.roll(r, 2 * hg, 1)))

    mx = over_groups(lse, jnp.maximum)
    e = jnp.where(used, jnp.exp(lse - mx), 0.0)
    alpha = jnp.where(used, e / over_groups(e, jnp.add), 0.0)
    a_hi, a_lo = _split2(alpha)
    packed = (a_hi.astype(F32) + pltpu.roll(a_lo.astype(F32), 4 * hg, 1)).astype(BF16)
    a_exp = jnp.dot(packed, e3_ref[...], preferred_element_type=F32)
    attn = (a_exp[:, :ATTN_OUT] * natural(o0_ref, None)
            + a_exp[:, ATTN_OUT:2 * ATTN_OUT] * natural(o1_ref, ob1_ref)
            + a_exp[:, 2 * ATTN_OUT:] * natural(o2_ref, ob2_ref))
    mix = (jnp.dot(attn.astype(BF16), wa_ref[...], preferred_element_type=F32)
           + jnp.dot(ssm_ref[0], ws_ref[...], preferred_element_type=F32))
    y_ref[0] = x_ref[0] + _rms_scale(mix) * nw_ref[...]


def _outproj(outs, lses, ssm, x3d, wa, ws, nw, job=None):
    b, s, _ = x3d.shape
    assert outs[0].shape[1] == 1
    tm = min(ROW_TILE, s)
    tpb = s // tm
    k = jnp.arange(LANES)
    slot = jnp.where(k % (4 * ATTN_HG) < N_PATTERNS * ATTN_HG, k % (4 * ATTN_HG), -1)
    slot = jnp.where(k < 8 * ATTN_HG, slot, -1)
    e3 = (slot[:, None] == (jnp.arange(N_PATTERNS * ATTN_OUT) // HEAD_DIM)[None, :]).astype(BF16)
    row = lambda w: pl.BlockSpec((1, tm, w), lambda i: (i // tpb, i % tpb, 0))
    grp = lambda a: pl.BlockSpec((1, a.shape[1], tm // a.shape[1], a.shape[3]), lambda i: (i // tpb, 0, i % tpb, 0))
    (y,), cache_res = _row_tile_call(
        _outproj_kernel, job,
        grid=(b * tpb,),
        in_specs=[grp(a) for a in outs] + [grp(a) for a in lses] + [row(SSD_INNER), row(D_MODEL),
                  _const_spec(e3.shape), _const_spec(wa.shape), _const_spec(ws.shape), _const_spec(nw.shape)],
        inputs=(*outs, *lses, ssm, x3d, e3, wa, ws, nw),
        out_specs=[row(D_MODEL)],
        out_shape=[jax.ShapeDtypeStruct((b, s, D_MODEL), F32)],
        scratch_shapes=[pltpu.VMEM((ATTN_OUT // LANES, tm, LANES), F32), pltpu.VMEM((ATTN_OUT // LANES, tm, LANES), F32),
                        pltpu.VMEM((1, tm, LANES), F32), pltpu.VMEM((1, tm, LANES), F32)],
        name="outproj")
    return y, cache_res


def _ffn_kernel(y_ref, nw1_ref, wg_ref, wu_ref, wo_ref, nw2_ref, out_ref, xn_ref, acc_ref):
    y = y_ref[...]
    xn_ref[...] = (_rms_scale(y) * nw1_ref[...]).astype(BF16)
    xn = xn_ref[...]
    for j in range(FFN_HIDDEN // FFN_CHUNK):
        cols = slice(j * FFN_CHUNK, (j + 1) * FFN_CHUNK)
        gate = jnp.dot(xn, wg_ref[:, cols], preferred_element_type=F32)
        up = jnp.dot(xn, wu_ref[:, cols], preferred_element_type=F32)
        h = (gate * _sigmoid(gate) * up).astype(BF16)
        part = jnp.dot(h, wo_ref[cols, :], preferred_element_type=F32)
        if j == 0:
            acc_ref[...] = part
        else:
            acc_ref[...] += part
    out_ref[...] = y + _rms_scale(acc_ref[...]) * nw2_ref[...]


def _ffn(y2d, nw1, wg, wu, wo, nw2, job=None):
    m = y2d.shape[0]
    tm = min(ROW_TILE, m)
    row = pl.BlockSpec((tm, D_MODEL), lambda i: (i, 0))
    (out,), cache_res = _row_tile_call(
        _ffn_kernel, job,
        grid=(m // tm,),
        in_specs=[row, _const_spec(nw1.shape), _const_spec(wg.shape), _const_spec(wu.shape),
                  _const_spec(wo.shape), _const_spec(nw2.shape)],
        inputs=(y2d, nw1, wg, wu, wo, nw2),
        out_specs=[row],
        out_shape=[jax.ShapeDtypeStruct((m, D_MODEL), F32)],
        scratch_shapes=[pltpu.VMEM((tm, D_MODEL), BF16), pltpu.VMEM((tm, D_MODEL), F32)],
        name="ffn")
    return out, cache_res


def _rope_tables(pos):
    half = HEAD_DIM // 2
    inv_freq = ROPE_THETA ** (-jnp.arange(half, dtype=F32) / half)
    ang = pos.astype(F32)[:, None] * inv_freq[None, :]
    cos, sin = jnp.cos(ang), jnp.sin(ang)
    reps = LANES // HEAD_DIM
    return (jnp.tile(jnp.concatenate([cos, cos], axis=1), (1, reps)),
            jnp.tile(jnp.concatenate([-sin, sin], axis=1), (1, reps)))


def _pad_lanes(v):
    return jnp.pad(v.astype(F32), (0, LANES - v.shape[0])).reshape(1, LANES)


def kernel(x_prompt, x_sample, cache_kv_w128, cache_kv_w512, cache_kv_w2048, state_conv, state_ssm,
           norm_mix_pre, norm_mix_post, norm_ffn_pre, norm_ffn_post, w_in, w_out, conv_w, conv_b,
           dt_bias, a_log, d_skip, ssd_norm_w, w_ffn_in, w_ffn_out):
    b, s, _ = x_prompt.shape
    db, ds, _ = x_sample.shape
    assert ds == 1 and norm_mix_pre.shape[0] == 1, "one layer, one sample token"
    assert s % ROW_TILE == 0 and s % (WIN_KEYS * ATTN_PATTERNS[-1][1]) == 0
    past_len = 8192

    w_in0 = w_in[0]
    o1, o2, o3 = ATTN_QKV, ATTN_QKV + SSD_INNER, ATTN_QKV + SSD_INNER + SSD_CONV_DIM
    wqkv = w_in0[:, :o1].astype(BF16)
    wz = w_in0[:, o1:o2].astype(BF16)
    wxbc = w_in0[:, o2:o3].astype(BF16)
    wdt = jnp.pad(w_in0[:, o3:], ((0, 0), (0, LANES - SSD_HEADS))).astype(BF16)
    wa = w_out[0][:ATTN_OUT].astype(BF16)
    ws = w_out[0][ATTN_OUT:].astype(BF16)
    wg = w_ffn_in[0][:, :FFN_HIDDEN].astype(BF16)
    wu = w_ffn_in[0][:, FFN_HIDDEN:].astype(BF16)
    wo = w_ffn_out[0].astype(BF16)
    nw_mix_pre, nw_mix_post = norm_mix_pre.reshape(1, -1), norm_mix_post.reshape(1, -1)
    nw_ffn_pre, nw_ffn_post = norm_ffn_pre.reshape(1, -1), norm_ffn_post.reshape(1, -1)
    ssd_nw = ssd_norm_w.reshape(1, -1)
    cw, cb = conv_w[0], conv_b.reshape(1, -1)
    dtb, alog = _pad_lanes(dt_bias[0]), _pad_lanes(a_log[0])
    dskip_row = jnp.repeat(d_skip[0].astype(F32), SSD_HEADDIM).reshape(1, SSD_INNER)
    caches = (cache_kv_w128[0], cache_kv_w512[0], cache_kv_w2048[0])

    xs3 = x_sample.reshape(1, db, D_MODEL)
    cos_s, sin_s = _rope_tables(jnp.full((db,), past_len))
    (*qkv_s, z_s, xbc_s, dt_s), _ = _inproj(xs3, nw_mix_pre, cos_s, sin_s, wqkv, wz, wxbc, wdt, (1,) * N_PATTERNS)
    qkv_s = [q.reshape(db, 3 * ATTN_OUT) for q in qkv_s]

    big = N_PATTERNS - 1
    ride_along = b * (s // min(ROW_TILE, s)) == db
    head_slices = {"inproj": (0, 2), "ffn": (2, 2), "outproj": (4, 4)}
    job = lambda key, partial: (q4_big, c_big, partial, ATTN_PATTERNS[big][1]) + head_slices[key] if ride_along else None
    if ride_along:
        c_big, q4_big = _sample_cache_inputs(qkv_s[big], caches[big])

    dils = tuple(d for _, d in ATTN_PATTERNS)
    cos_p, sin_p = _rope_tables(jnp.arange(s))
    (*qkv_groups, z, xbc, dt_raw), ride0 = _inproj(x_prompt, nw_mix_pre, cos_p, sin_p, wqkv, wz, wxbc, wdt, dils,
                                                    job("inproj", None))
    outs, lses = zip(*[_prompt_attention(qkv_groups[g], g) for g in range(N_PATTERNS)])
    ssm, p_conv, h_t = _prompt_ssd(xbc, z, dt_raw, cw, cb, dtb, alog, dskip_row, ssd_nw, b, s)
    y1, ride1 = _outproj(outs, lses, ssm, x_prompt, wa, ws, nw_mix_post, job("outproj", ride0 and ride0[0]))
    y_prompt, ride2 = _ffn(y1.reshape(b * s, D_MODEL), nw_ffn_pre, wg, wu, wo, nw_ffn_post,
                           job("ffn", ride1 and ride1[0]))
    y_prompt = y_prompt.reshape(b, s, D_MODEL)

    p_kv = [_prompt_kv_tail(qkv_groups[g], window, s) for g, (window, _) in enumerate(ATTN_PATTERNS)]
    p_ssm = h_t.reshape(b, SSD_STATE, SSD_HEADS, SSD_HEADDIM).transpose(0, 2, 3, 1)[None]

    s_kv, outs_s, lses_s = [], [], []
    for g in range(N_PATTERNS):
        if g == big and ride_along:
            rows = jnp.concatenate([r[1][:, 0, :head_slices[key][1]] for key, r in
                                    (("inproj", ride0), ("ffn", ride2), ("outproj", ride1))], axis=1)
            kv_new, o_g, lse_g = _sample_cache_outputs(ride2[0], rows)
        else:
            kv_new, o_g, lse_g = _sample_cache_step(qkv_s[g], caches[g], g)
        s_kv.append(kv_new)
        outs_s.append(o_g.astype(BF16).reshape(1, 1, db, ATTN_OUT))
        lse_g = jnp.pad(lse_g, ((0, 0), (g * ATTN_HG, LANES - (g + 1) * ATTN_HG)))
        lses_s.append(lse_g.reshape(1, 1, db, LANES))
    ssm_s, sconv_new, h_new = _sample_ssd(xbc_s[0], z_s[0], dt_s[0], state_conv[0], state_ssm[0], cw, cb, dtb, alog,
                                          dskip_row, ssd_nw)
    y1_s, _ = _outproj(outs_s, lses_s, ssm_s[None], xs3, wa, ws, nw_mix_post)
    y_sample = _ffn(y1_s[0], nw_ffn_pre, wg, wu, wo, nw_ffn_post)[0].reshape(db, 1, D_MODEL)

    s_conv = sconv_new.reshape(1, db, SSD_CONV - 1, SSD_CONV_DIM)
    s_ssm = h_new.reshape(1, db, SSD_HEADS, SSD_HEADDIM, SSD_STATE)
    return (y_prompt, y_sample, p_kv[0], p_kv[1], p_kv[2], p_conv[None], p_ssm,
            s_kv[0], s_kv[1], s_kv[2], s_conv, s_ssm)
```

```python
import functools

import jax
import jax.numpy as jnp
from jax import lax
from jax.experimental import pallas as pl
from jax.experimental.pallas import tpu as pltpu

F32 = jnp.float32
BF16 = jnp.bfloat16

D_MODEL = 1024
HEAD_DIM = 64
ATTN_PATTERNS = ((128, 1), (512, 4), (2048, 16))
N_PATTERNS = len(ATTN_PATTERNS)
ATTN_HG = 8
WIN_KEYS = 128
ATTN_OUT = ATTN_HG * HEAD_DIM
ATTN_QKV = N_PATTERNS * 3 * ATTN_OUT
ROPE_THETA = 10000.0
ATTN_SCALE = HEAD_DIM ** -0.5
NEG_INF = -1e30
SSD_INNER = D_MODEL
SSD_HEADDIM = 64
SSD_HEADS = SSD_INNER // SSD_HEADDIM
SSD_GROUPS = 2
SSD_HPG = SSD_HEADS // SSD_GROUPS
SSD_STATE = 128
SSD_CONV = 4
SSD_CONV_DIM = SSD_INNER + 2 * SSD_GROUPS * SSD_STATE
FFN_HIDDEN = 2816
NORM_EPS = 1e-6

LANES = 128
SUBLANES = 8
ROW_TILE = 512
SSD_T = 128
FFN_CHUNK = 256
VMEM_LIMIT = 56 * 1024 * 1024
CACHE_BLOCK_BYTES = 4 * 1024 * 1024


def _cparams(sem):
    return pltpu.CompilerParams(dimension_semantics=sem, vmem_limit_bytes=VMEM_LIMIT)


def _const_spec(shape):
    nd = len(shape)
    return pl.BlockSpec(shape, lambda *_: (0,) * nd, pipeline_mode=pl.Buffered(1))


def _split2(x):
    hi = x.astype(BF16)
    lo = (x - hi.astype(F32)).astype(BF16)
    return hi, lo


def _split3(x):
    hi = x.astype(BF16)
    r = x - hi.astype(F32)
    mid = r.astype(BF16)
    lo = (r - mid.astype(F32)).astype(BF16)
    return hi, mid, lo


def _sigmoid(x):
    return 1.0 / (1.0 + jnp.exp(-x))


def _softplus(x):
    return jnp.maximum(x, 0.0) + jnp.log(1.0 + jnp.exp(-jnp.abs(x)))


def _rms_scale(x):
    return x * lax.rsqrt(jnp.mean(x * x, axis=-1, keepdims=True) + NORM_EPS)


def _with_cache_jobs(kernel_fn, n_in, n_out, n_scratch, job_params):
    def wrapped(*refs):
        ins, rest = refs[:n_in], refs[n_in:]
        c_ins = []
        for n_cache_in, _, _, _ in job_params:
            c_ins.append(rest[:n_cache_in])
            rest = rest[n_cache_in:]
        outs, rest = rest[:n_out], rest[n_out:]
        c_outs, rest = rest[:2 * len(job_params)], rest[2 * len(job_params):]
        scratch, col_refs = rest[:n_scratch], rest[n_scratch:]
        for j, (_, dil, head_lo, hv) in enumerate(job_params):
            _sample_cache_body(c_ins[j][0], c_ins[j][1], c_outs[2 * j], c_outs[2 * j + 1], col_refs[j], head_lo, dil, hv)
        kernel_fn(*ins, *outs, *scratch)
    return wrapped


def _row_tile_call(kernel_fn, jobs, *, grid, in_specs, inputs, out_specs, out_shape, scratch_shapes, name):
    in_specs, inputs = list(in_specs), list(inputs)
    out_specs, out_shape, scratch_shapes = list(out_specs), list(out_shape), list(scratch_shapes)
    n_in, n_out, n_scratch = len(in_specs), len(out_specs), len(scratch_shapes)
    job_params, cache_blocks, aliases = [], [], {}
    for q4, cache_t, partial, dil, head_lo, n_heads in jobs:
        db, _, _, hd, wb = cache_t.shape
        assert grid == (db,) and head_lo % n_heads == 0
        cblk = pl.BlockSpec((1, 2, n_heads, hd, wb), lambda i, blk=head_lo // n_heads: (i, 0, blk, 0, 0))
        cache_blocks.append(cblk)
        in_specs += [pl.BlockSpec((1, 3, ATTN_HG, LANES), lambda i: (i, 0, 0, 0)), cblk]
        inputs += [q4, cache_t]
        if partial is not None:
            aliases[len(in_specs)] = n_out + 2 * len(job_params)
            in_specs.append(pl.BlockSpec(memory_space=pl.ANY))
            inputs.append(partial)
        hv = n_heads if n_heads == ATTN_HG and 2 * n_heads * hd * wb * 4 <= CACHE_BLOCK_BYTES // 2 else 1
        job_params.append((2 if partial is None else 3, dil, head_lo, hv))
    for (_, cache_t, *_), cblk in zip(jobs, cache_blocks):
        out_specs += [cblk, pl.BlockSpec((1, 1, SUBLANES, LANES), lambda i: (i, 0, 0, 0))]
        out_shape += [jax.ShapeDtypeStruct(cache_t.shape, F32),
                      jax.ShapeDtypeStruct((cache_t.shape[0], 1, SUBLANES, LANES), F32)]
        scratch_shapes.append(pltpu.VMEM((3, ATTN_HG, cache_t.shape[3], 1), F32))
    body = _with_cache_jobs(kernel_fn, n_in, n_out, n_scratch, job_params) if jobs else kernel_fn
    res = pl.pallas_call(body, grid=grid, in_specs=in_specs, out_specs=out_specs, out_shape=out_shape,
                         scratch_shapes=scratch_shapes, input_output_aliases=aliases,
                         compiler_params=_cparams(("arbitrary",)), name=name)(*inputs)
    return res[:n_out], [(res[n_out + 2 * j], res[n_out + 2 * j + 1]) for j in range(len(jobs))]


def _inproj_kernel(x_ref, nw_ref, cos_ref, sin_ref, wqkv_ref, wz_ref, wxbc_ref, wdt_ref,
                   qkv0_ref, qkv1_ref, qkv2_ref, z_ref, xbc_ref, dt_ref, xn_ref, perm_ref, perm4_ref, *, dils):
    tm = x_ref.shape[1]
    xn_ref[...] = (_rms_scale(x_ref[0]) * nw_ref[...]).astype(BF16)
    xn = xn_ref[...]
    cos = cos_ref[...]
    sin = sin_ref[...]
    lane = lax.broadcasted_iota(jnp.int32, (tm, LANES), 1)
    first_half = (lane % HEAD_DIM) < (HEAD_DIM // 2)
    group_refs = (qkv0_ref, qkv1_ref, qkv2_ref)
    for j in range(ATTN_QKV // ATTN_OUT):
        group, part = divmod(j, 3)
        dil = dils[group]
        out_ref = group_refs[group]
        acc = jnp.dot(xn, wqkv_ref[:, j * ATTN_OUT:(j + 1) * ATTN_OUT], preferred_element_type=F32)
        for c in range(ATTN_OUT // LANES):
            a = acc[:, c * LANES:(c + 1) * LANES]
            if part != 2:
                partner = jnp.where(first_half, pltpu.roll(a, LANES - HEAD_DIM // 2, 1),
                                    pltpu.roll(a, HEAD_DIM // 2, 1))
                a = a * cos + partner * sin
                if part == 0:
                    a = a * ATTN_SCALE
            cols = slice(part * ATTN_OUT + c * LANES, part * ATTN_OUT + (c + 1) * LANES)
            if dil == 1:
                out_ref[0, 0, :, cols] = a.astype(BF16)
            else:
                perm_ref[...] = a
                if dil <= 4:
                    for r in range(dil):
                        out_ref[0, r, :, cols] = perm_ref[pl.ds(r, tm // dil, stride=dil), :].astype(BF16)
                else:
                    for r_lo in range(4):
                        perm4_ref[r_lo] = perm_ref[pl.ds(r_lo, tm // 4, stride=4), :]
                    for r_lo in range(4):
                        for r_hi in range(dil // 4):
                            out_ref[0, 4 * r_hi + r_lo, :, cols] = (
                                perm4_ref[r_lo, pl.ds(r_hi, tm // dil, stride=dil // 4), :].astype(BF16))
    z_ref[0] = jnp.dot(xn, wz_ref[...], preferred_element_type=F32).astype(BF16)
    xbc_ref[0] = jnp.dot(xn, wxbc_ref[...], preferred_element_type=F32).astype(BF16)
    dt_ref[0] = jnp.dot(xn, wdt_ref[...], preferred_element_type=F32)


def _inproj(x3d, nw, cos_t, sin_t, wqkv, wz, wxbc, wdt, dils, jobs=()):
    b, s, _ = x3d.shape
    tm = min(ROW_TILE, s)
    tpb = s // tm
    assert cos_t.shape[0] == s
    row = lambda w: pl.BlockSpec((1, tm, w), lambda i: (i // tpb, i % tpb, 0))
    tab = pl.BlockSpec((tm, LANES), lambda i: (i % tpb, 0))
    grp = lambda d: pl.BlockSpec((1, d, tm // d, 3 * ATTN_OUT), lambda i: (i // tpb, 0, i % tpb, 0))
    return _row_tile_call(
        functools.partial(_inproj_kernel, dils=dils), jobs,
        grid=(b * tpb,),
        in_specs=[row(D_MODEL), _const_spec((1, D_MODEL)), tab, tab,
                  _const_spec(wqkv.shape), _const_spec(wz.shape), _const_spec(wxbc.shape),
                  _const_spec(wdt.shape)],
        inputs=(x3d, nw, cos_t, sin_t, wqkv, wz, wxbc, wdt),
        out_specs=[grp(d) for d in dils] + [row(SSD_INNER), row(SSD_CONV_DIM), row(LANES)],
        out_shape=[jax.ShapeDtypeStruct((b, d, s // d, 3 * ATTN_OUT), BF16) for d in dils]
                  + [jax.ShapeDtypeStruct((b, s, SSD_INNER), BF16),
                     jax.ShapeDtypeStruct((b, s, SSD_CONV_DIM), BF16),
                     jax.ShapeDtypeStruct((b, s, LANES), F32)],
        scratch_shapes=[pltpu.VMEM((tm, D_MODEL), BF16), pltpu.VMEM((tm, LANES), F32),
                        pltpu.VMEM((4, tm // 4, LANES), F32)],
        name="inproj")


def _attn_kernel(q_ref, kc_ref, vc_ref, kp_ref, vp_ref, o_ref, lse_ref, k_scr, v_scr, *, group):
    cq = q_ref.shape[0]
    w = WIN_KEYS
    n = pl.program_id(2)
    k_scr[0:w] = kp_ref[...]
    k_scr[w:] = kc_ref[...]
    v_scr[0:w] = vp_ref[...]
    v_scr[w:] = vc_ref[...]
    lane = lax.broadcasted_iota(jnp.int32, (w, LANES), 1)
    lo = lane < HEAD_DIM
    qi = lax.broadcasted_iota(jnp.int32, (w, 2 * w), 0)
    ki = lax.broadcasted_iota(jnp.int32, (w, 2 * w), 1)
    band = ((ki < w) & (ki >= qi)) | ((ki >= w) & ((ki - w) <= qi))
    for i in range(cq // w):
        if i == 0:
            kmin = jnp.where(n > 0, 0, w)
            valid = band & (ki >= kmin)
        else:
            valid = band
        lse_tile = jnp.zeros((w, LANES), F32)
        for pair in range(ATTN_OUT // LANES):
            cols = slice(pair * LANES, (pair + 1) * LANES)
            qp = q_ref[i * w:(i + 1) * w, cols]
            kk = k_scr[i * w:(i + 2) * w, cols]
            vv = v_scr[i * w:(i + 2) * w, cols]
            halves = []
            for half in range(2):
                qh = jnp.where(lo if half == 0 else jnp.logical_not(lo), qp, jnp.zeros_like(qp))
                s = lax.dot_general(qh, kk, (((1,), (1,)), ((), ())), preferred_element_type=F32)
                s = jnp.where(valid, s, NEG_INF)
                m = jnp.max(s, axis=-1, keepdims=True)
                p = jnp.exp(s - m)
                den = jnp.sum(p, axis=-1, keepdims=True)
                pv = jnp.dot(p.astype(BF16), vv, preferred_element_type=F32)
                halves.append(pv / den)
                head = 2 * pair + half
                lse_tile = jnp.where(lane == group * ATTN_HG + head, m + jnp.log(den), lse_tile)
            o_ref[i * w:(i + 1) * w, cols] = jnp.where(lo, halves[0], halves[1]).astype(BF16)
        lse_ref[i * w:(i + 1) * w, :] = lse_tile


def _prompt_attention(qkv_g, group):
    b, dil, sub_len, _ = qkv_g.shape
    cq = min(ROW_TILE, sub_len)
    blk_per_chunk = cq // WIN_KEYS
    cur = lambda t: pl.BlockSpec((None, None, cq, ATTN_OUT), lambda bi, r, n: (bi, r, n, t))
    prev = lambda t: pl.BlockSpec(
        (None, None, WIN_KEYS, ATTN_OUT),
        lambda bi, r, n: (bi, r, jnp.maximum(n * blk_per_chunk - 1, 0), t))
    out = lambda wdt: pl.BlockSpec((None, None, cq, wdt), lambda bi, r, n: (bi, r, n, 0))
    return pl.pallas_call(
        functools.partial(_attn_kernel, group=group),
        grid=(b, dil, sub_len // cq),
        in_specs=[cur(0), cur(1), cur(2), prev(1), prev(2)],
        out_specs=[out(ATTN_OUT), out(LANES)],
        out_shape=[jax.ShapeDtypeStruct((b, dil, sub_len, ATTN_OUT), BF16),
                   jax.ShapeDtypeStruct((b, dil, sub_len, LANES), F32)],
        scratch_shapes=[pltpu.VMEM((cq + WIN_KEYS, ATTN_OUT), BF16),
                        pltpu.VMEM((cq + WIN_KEYS, ATTN_OUT), BF16)],
        compiler_params=_cparams(("arbitrary", "arbitrary", "arbitrary")),
        name=f"prompt_attn_g{group}",
    )(qkv_g, qkv_g, qkv_g, qkv_g, qkv_g)


def _pack3_lanes(x):
    hi, mid, lo = _split3(x)
    packed = hi.astype(F32) + pltpu.roll(mid.astype(F32), SSD_HEADS, 1) + pltpu.roll(lo.astype(F32), 2 * SSD_HEADS, 1)
    return packed.astype(BF16)


def _ssd_kernel(xbc_ref, z_ref, dtr_ref, convw_ref, convb_ref, dtb_ref, alog_ref, dskip_ref, nw_ref,
                tri3_ref, rexp_ref, eexp_ref, shift_ref,
                y_ref, pconv_ref, hT_out_ref,
                x2_ref, hT_ref, csb_ref, ybuf_ref):
    t = SSD_T
    c = pl.program_id(1)
    last = pl.num_programs(1) - 1
    taps = SSD_CONV

    @pl.when(c == 0)
    def _():
        x2_ref[0:t] = jnp.zeros((t, SSD_CONV_DIM), BF16)
        hT_ref[...] = jnp.zeros_like(hT_ref)

    cur = xbc_ref[0]
    x2_ref[t:] = cur
    cur_f = cur.astype(F32)
    acc = convb_ref[...] + convw_ref[taps - 1:taps, :] * cur_f
    for j in range(taps - 1):
        shifted = jnp.dot(shift_ref[j], x2_ref[...], preferred_element_type=F32)
        acc = acc + convw_ref[j:j + 1, :] * shifted

    @pl.when(c == last)
    def _():
        pconv_ref[0] = cur_f[t - (taps - 1):, :]

    x2_ref[0:t] = cur
    xc = acc * _sigmoid(acc)
    xs = xc[:, :SSD_INNER]
    xs_b = xs.astype(BF16)
    gn = SSD_GROUPS * SSD_STATE

    lane = lax.broadcasted_iota(jnp.int32, (t, LANES), 1)
    dt = jnp.where(lane < SSD_HEADS, _softplus(dtr_ref[0] + dtb_ref[...]), 0.0)
    a = dt * (-jnp.exp(alog_ref[...]))
    a3 = jnp.concatenate(_split3(a), axis=0)
    cs = jnp.dot(tri3_ref[...], a3, preferred_element_type=F32)
    cs_last = cs[t - 1:t, :]
    wl = jnp.exp(cs_last - cs) * dt
    csT = cs.T
    dtT = dt.T
    wlT = wl.T
    csb_ref[...] = jnp.dot(_pack3_lanes(cs), rexp_ref[...], preferred_element_type=F32)
    cs_last8 = jnp.broadcast_to(cs_last, (SUBLANES, LANES))
    dec = jnp.exp(jnp.dot(_pack3_lanes(cs_last8), eexp_ref[...], preferred_element_type=F32)[0:1, :])

    ti = lax.broadcasted_iota(jnp.int32, (t, t), 0)
    si = lax.broadcasted_iota(jnp.int32, (t, t), 1)
    tri = si <= ti
    lo = lane < SSD_HEADDIM

    g_mat, bT, yoff = [], [], []
    for g in range(SSD_GROUPS):
        bm = xc[:, SSD_INNER + g * SSD_STATE:SSD_INNER + (g + 1) * SSD_STATE]
        cm = xc[:, SSD_INNER + gn + g * SSD_STATE:SSD_INNER + gn + (g + 1) * SSD_STATE].astype(BF16)
        g_mat.append(lax.dot_general(cm, bm.astype(BF16), (((1,), (1,)), ((), ())),
                                     preferred_element_type=F32))
        bT.append(bm.T)
        cols = slice(g * SSD_HPG * SSD_HEADDIM, (g + 1) * SSD_HPG * SSD_HEADDIM)
        yoff.append(jnp.dot(cm, hT_ref[:, cols].astype(BF16), preferred_element_type=F32))

    ss = jnp.zeros((t, 1), F32)
    pairs_per_group = SSD_HPG // 2
    for pair in range(SSD_HEADS // 2):
        g = pair // pairs_per_group
        cols = slice(pair * LANES, (pair + 1) * LANES)
        w_blocks, b_blocks = [], []
        for h in (2 * pair, 2 * pair + 1):
            seg = csb_ref[:, h * LANES:(h + 1) * LANES] - csT[h:h + 1, :]
            seg = jnp.where(tri, seg, NEG_INF)
            w_blocks.append((g_mat[g] * jnp.exp(seg) * dtT[h:h + 1, :]).astype(BF16))
            b_blocks.append((bT[g] * wlT[h:h + 1, :]).astype(BF16))
        lhs = jnp.concatenate([jnp.concatenate(w_blocks, axis=1), jnp.concatenate(b_blocks, axis=1)], axis=0)
        xp = xs_b[:, cols]
        zero = jnp.zeros_like(xp)
        xbd = jnp.concatenate([jnp.where(lo, xp, zero), jnp.where(lo, zero, xp)], axis=0)
        res = jnp.dot(lhs, xbd, preferred_element_type=F32)
        ecs = jnp.exp(jnp.where(lo, csb_ref[:, 2 * pair * LANES:(2 * pair + 1) * LANES],
                                csb_ref[:, (2 * pair + 1) * LANES:(2 * pair + 2) * LANES]))
        gcol = (pair % pairs_per_group) * LANES
        y = res[:t] + yoff[g][:, gcol:gcol + LANES] * ecs + dskip_ref[:, cols] * xs[:, cols]
        hT_ref[:, cols] = hT_ref[:, cols] * dec[:, cols] + res[t:]
        zf = z_ref[0, :, cols].astype(F32)
        gated = y * (zf * _sigmoid(zf))
        ybuf_ref[:, cols] = gated
        ss = ss + jnp.sum(gated * gated, axis=-1, keepdims=True)

    y_ref[0] = (ybuf_ref[...] * lax.rsqrt(ss * (1.0 / SSD_INNER) + NORM_EPS) * nw_ref[...]).astype(BF16)

    @pl.when(c == last)
    def _():
        hT_out_ref[0] = hT_ref[...]


def _ssd_consts():
    t = SSD_T
    tri = (jnp.arange(t)[:, None] >= jnp.arange(t)[None, :]).astype(BF16)
    tri3 = jnp.concatenate([tri, tri, tri], axis=1)
    k = jnp.arange(LANES)
    piece_head = jnp.where(k < 3 * SSD_HEADS, k % SSD_HEADS, -1)
    rexp = (piece_head[:, None] == (jnp.arange(SSD_HEADS * LANES) // LANES)[None, :]).astype(BF16)
    eexp = (piece_head[:, None] == (jnp.arange(SSD_INNER) // SSD_HEADDIM)[None, :]).astype(BF16)
    return tri3, rexp, eexp


def _prompt_ssd(xbc, z, dt_raw, conv_w, conv_b, dt_bias128, a_log128, dskip_row, ssd_nw, b, s):
    t = SSD_T
    tri3, rexp, eexp = _ssd_consts()
    back = (SSD_CONV - 1 - jnp.arange(SSD_CONV - 1))[:, None, None]
    shift = (jnp.arange(2 * t)[None, None, :] == t + jnp.arange(t)[None, :, None] - back).astype(BF16)
    tok = lambda w: pl.BlockSpec((1, t, w), lambda bi, c: (bi, c, 0))
    per_b = lambda *shape: pl.BlockSpec((1,) + shape, lambda bi, c: (bi,) + (0,) * len(shape))
    return pl.pallas_call(
        _ssd_kernel,
        grid=(b, s // t),
        in_specs=[tok(SSD_CONV_DIM), tok(SSD_INNER), tok(LANES),
                  _const_spec(conv_w.shape), _const_spec(conv_b.shape), _const_spec(dt_bias128.shape),
                  _const_spec(a_log128.shape), _const_spec(dskip_row.shape), _const_spec(ssd_nw.shape),
                  _const_spec(tri3.shape), _const_spec(rexp.shape), _const_spec(eexp.shape),
                  _const_spec(shift.shape)],
        out_specs=[tok(SSD_INNER), per_b(SSD_CONV - 1, SSD_CONV_DIM), per_b(SSD_STATE, SSD_INNER)],
        out_shape=[jax.ShapeDtypeStruct((b, s, SSD_INNER), BF16),
                   jax.ShapeDtypeStruct((b, SSD_CONV - 1, SSD_CONV_DIM), F32),
                   jax.ShapeDtypeStruct((b, SSD_STATE, SSD_INNER), F32)],
        scratch_shapes=[pltpu.VMEM((2 * t, SSD_CONV_DIM), BF16),
                        pltpu.VMEM((SSD_STATE, SSD_INNER), F32),
                        pltpu.VMEM((t, SSD_HEADS * LANES), F32),
                        pltpu.VMEM((t, SSD_INNER), F32)],
        compiler_params=_cparams(("arbitrary", "arbitrary")),
        name="prompt_ssd",
    )(xbc.reshape(b, s, SSD_CONV_DIM), z.reshape(b, s, SSD_INNER), dt_raw.reshape(b, s, LANES),
      conv_w, conv_b, dt_bias128, a_log128, dskip_row, ssd_nw, tri3, rexp, eexp, shift)


def _pkv_kernel(k_ref, v_ref, out_ref, buf_ref):
    _, dil, rows, _ = k_ref.shape
    tb = dil * rows
    for part, ref in enumerate((k_ref, v_ref)):
        for c in range(ATTN_OUT // LANES):
            cols = slice(c * LANES, (c + 1) * LANES)
            if dil == 1:
                nat = ref[0, 0, :, cols].astype(F32)
            else:
                for r in range(dil):
                    buf_ref[pl.ds(r, rows, stride=dil), :] = ref[0, r, :, cols].astype(F32)
                nat = buf_ref[...]
            base = part * ATTN_OUT + c * LANES
            for tblk in range(tb // LANES):
                out_ref[0, base:base + LANES, tblk * LANES:(tblk + 1) * LANES] = nat[tblk * LANES:(tblk + 1) * LANES, :].T


def _prompt_kv_tail(qkv_g, window, s):
    b, dil, _, _ = qkv_g.shape
    wlen = min(window, s)
    tb = min(ROW_TILE, wlen)
    assert (s - wlen) % tb == 0 and tb % (dil * 2 * SUBLANES) == 0
    first = (s - wlen) // tb
    blk = lambda t: pl.BlockSpec((1, dil, tb // dil, ATTN_OUT), lambda bi, n: (bi, 0, first + n, t))
    out = pl.pallas_call(
        _pkv_kernel,
        grid=(b, wlen // tb),
        in_specs=[blk(1), blk(2)],
        out_specs=pl.BlockSpec((1, 2 * ATTN_OUT, tb), lambda bi, n: (bi, 0, n)),
        out_shape=jax.ShapeDtypeStruct((b, 2 * ATTN_OUT, wlen), F32),
        scratch_shapes=[pltpu.VMEM((tb, LANES), F32)],
        compiler_params=_cparams(("arbitrary", "arbitrary")),
        name=f"prompt_kv_tail_w{window}",
    )(qkv_g, qkv_g)
    return jnp.transpose(out.reshape(1, b, 2, ATTN_HG, HEAD_DIM, wlen), (0, 1, 5, 2, 3, 4))


def _sample_cache_kernel(q_ref, c_ref, co_ref, o_ref, col_ref, *, dil, hv):
    _sample_cache_body(q_ref, c_ref, co_ref, o_ref, col_ref, pl.program_id(1) * c_ref.shape[2], dil, hv)


def _sample_cache_body(q_ref, c_ref, co_ref, o_ref, col_ref, head0, dil, hv):
    bb, _, hb, hd, wb = c_ref.shape
    pos = lax.broadcasted_iota(jnp.int32, (1, 1, wb), 2)
    in_window = (pos & (dil - 1)) == 0
    is_last = lax.broadcasted_iota(jnp.int32, (1, hd, wb), 2) == wb - 1
    lane = lax.broadcasted_iota(jnp.int32, (hd, LANES), 1)
    lane_row = lax.broadcasted_iota(jnp.int32, (1, LANES), 1)
    for bi in range(bb):
        for part in range(3):
            cols = q_ref[bi, part].T
            for h in range(ATTN_HG):
                col_ref[part, h] = cols[0:hd, h:h + 1]
        o_cols = jnp.zeros((hd, LANES), F32)
        lse_row = jnp.zeros((1, LANES), F32)
        for h in range(0, hb, hv):
            q = col_ref[0, pl.ds(head0 + h, hv)]
            k_new = col_ref[1, pl.ds(head0 + h, hv)]
            v_new = col_ref[2, pl.ds(head0 + h, hv)]
            keys = c_ref[bi, 0, h:h + hv]
            vals = c_ref[bi, 1, h:h + hv]
            sc = jnp.where(in_window, jnp.sum(keys * q, axis=1, keepdims=True), NEG_INF)
            sc_new = jnp.sum(k_new * q, axis=1, keepdims=True)
            m = jnp.maximum(jnp.max(sc, axis=2, keepdims=True), sc_new)
            p = jnp.exp(sc - m)
            p_new = jnp.exp(sc_new - m)
            den = jnp.sum(p, axis=2, keepdims=True) + p_new
            out = (jnp.sum(vals * p, axis=2, keepdims=True) + v_new * p_new) / den
            lse = m + jnp.log(den)
            for t in range(hv):
                o_cols = jnp.where(lane == h + t, out[t], o_cols)
                lse_row = jnp.where(lane_row == h + t, lse[t], lse_row)
            co_ref[bi, 0, h:h + hv] = jnp.where(is_last, k_new, pltpu.roll(keys, wb - 1, 2))
            co_ref[bi, 1, h:h + hv] = jnp.where(is_last, v_new, pltpu.roll(vals, wb - 1, 2))
        tile = jnp.concatenate([o_cols, jnp.broadcast_to(lse_row, (SUBLANES, LANES)),
                                jnp.zeros((LANES - hd - SUBLANES, LANES), F32)], axis=0)
        o_ref[bi, 0] = tile.T[0:SUBLANES, :]


def _sample_cache_step(qkv_g, cache, group):
    db = qkv_g.shape[0]
    window, dil = ATTN_PATTERNS[group]
    wb = cache.shape[1]
    assert wb == window and dil & (dil - 1) == 0, "sample path expects a full window of cached rows"
    c_t, q4 = _sample_cache_inputs(qkv_g, cache)
    row_bytes = 2 * HEAD_DIM * wb * 4
    hb = max(1, min(ATTN_HG, CACHE_BLOCK_BYTES // row_bytes))
    bb = max(1, min(SUBLANES, CACHE_BLOCK_BYTES // (row_bytes * ATTN_HG)))
    hv = hb if hb == ATTN_HG else 1
    n_hblk = ATTN_HG // hb
    cblk = pl.BlockSpec((bb, 2, hb, HEAD_DIM, wb), lambda i, j: (i, 0, j, 0, 0))
    co, o = pl.pallas_call(
        functools.partial(_sample_cache_kernel, dil=dil, hv=hv),
        grid=(db // bb, n_hblk),
        in_specs=[pl.BlockSpec((bb, 3, ATTN_HG, LANES), lambda i, j: (i, 0, 0, 0)), cblk],
        out_specs=[cblk, pl.BlockSpec((bb, 1, SUBLANES, LANES), lambda i, j: (i, j, 0, 0))],
        out_shape=[jax.ShapeDtypeStruct(c_t.shape, F32),
                   jax.ShapeDtypeStruct((db, n_hblk, SUBLANES, LANES), F32)],
        scratch_shapes=[pltpu.VMEM((3, ATTN_HG, HEAD_DIM, 1), F32)],
        compiler_params=_cparams(("arbitrary", "arbitrary")),
        name=f"sample_cache_g{group}",
    )(q4, c_t)
    return _sample_cache_outputs(co, o[:, :, :hb].reshape(db, ATTN_HG, LANES))


def _sample_cache_inputs(qkv_g, cache):
    db = qkv_g.shape[0]
    c_t = jnp.transpose(cache, (0, 2, 3, 4, 1))
    q4 = jnp.pad(qkv_g.astype(F32).reshape(db, 3, ATTN_HG, HEAD_DIM), ((0, 0), (0, 0), (0, 0), (0, LANES - HEAD_DIM)))
    return c_t, q4


def _sample_cache_outputs(co, rows):
    db = rows.shape[0]
    return jnp.transpose(co, (0, 4, 1, 2, 3))[None], rows[:, :, :HEAD_DIM].reshape(db, ATTN_OUT), rows[:, :, HEAD_DIM]


def _sample_ssd_kernel(xbc_ref, z_ref, dtr_ref, sconv_ref, h_ref,
                       convw_ref, convb_ref, dtb_ref, alog_ref, dskip_ref, nw_ref, eexp_ref,
                       y_ref, sconv_out_ref, h_out_ref,
                       uT_ref, decT_ref, bm_ref, cT_ref, xs_ref, yT_ref):
    db = xbc_ref.shape[0]
    bb = h_ref.shape[0]
    i = pl.program_id(0)
    half = SSD_HPG * SSD_HEADDIM
    gn = SSD_GROUPS * SSD_STATE
    cd = SSD_CONV_DIM

    @pl.when(i == 0)
    def _():
        new = xbc_ref[...].astype(F32)
        acc = convb_ref[...] + convw_ref[SSD_CONV - 1:SSD_CONV, :] * new
        for j in range(SSD_CONV - 1):
            acc = acc + convw_ref[j:j + 1, :] * sconv_ref[:, j * cd:(j + 1) * cd]
        for j in range(1, SSD_CONV - 1):
            sconv_out_ref[:, (j - 1) * cd:j * cd] = sconv_ref[:, j * cd:(j + 1) * cd]
        sconv_out_ref[:, (SSD_CONV - 2) * cd:] = new
        xc = acc * _sigmoid(acc)
        xs = xc[:, :SSD_INNER]
        xs_ref[...] = xs
        lane = lax.broadcasted_iota(jnp.int32, (db, LANES), 1)
        dt = jnp.where(lane < SSD_HEADS, _softplus(dtr_ref[...] + dtb_ref[...]), 0.0)
        dec = jnp.exp(dt * (-jnp.exp(alog_ref[...])))
        dt_hi, dt_lo = _split2(dt)
        dt_e = (jnp.dot(dt_hi, eexp_ref[...], preferred_element_type=F32)
                + jnp.dot(dt_lo, eexp_ref[...], preferred_element_type=F32))
        dc_hi, dc_lo = _split2(dec)
        dec_e = (jnp.dot(dc_hi, eexp_ref[...], preferred_element_type=F32)
                 + jnp.dot(dc_lo, eexp_ref[...], preferred_element_type=F32))
        u = dt_e * xs
        for k in range(SSD_INNER // LANES):
            rows = slice(k * LANES, (k + 1) * LANES)
            uT_ref[rows, :] = u[:, rows].T.astype(BF16)
            d_hi, d_lo = _split2(dec_e[:, rows].T)
            decT_ref[rows, 0:db] = d_hi
            decT_ref[rows, db:2 * db] = d_lo
        for g in range(SSD_GROUPS):
            bm_ref[g] = xc[:, SSD_INNER + g * SSD_STATE:SSD_INNER + (g + 1) * SSD_STATE]
            cT_ref[g] = xc[:, SSD_INNER + gn + g * SSD_STATE:SSD_INNER + gn + (g + 1) * SSD_STATE].T
        yT_ref[...] = jnp.zeros_like(yT_ref)

    row_id = lax.broadcasted_iota(jnp.int32, (db, LANES), 0)
    col_id = lax.broadcasted_iota(jnp.int32, (SSD_STATE, db), 1)
    for j in range(bb):
        b = i * bb + j
        on_row = row_id == b
        sel = jnp.where(on_row, 1.0, 0.0).astype(BF16)
        sel2 = jnp.concatenate([sel, sel], axis=0)
        for g in range(SSD_GROUPS):
            rows = slice(g * half, (g + 1) * half)
            rhs_b = jnp.where(on_row, bm_ref[g], 0.0).astype(BF16)
            upd = jnp.dot(uT_ref[rows, :], rhs_b, preferred_element_type=F32)
            dec_rep = jnp.dot(decT_ref[rows, :], sel2, preferred_element_type=F32)
            hn = dec_rep * h_ref[j, rows, :] + upd
            h_out_ref[j, rows, :] = hn
            rhs_c = jnp.where(col_id == b, cT_ref[g], 0.0).astype(BF16)
            yT_ref[rows, :] += jnp.dot(hn.astype(BF16), rhs_c, preferred_element_type=F32)

    @pl.when(i == pl.num_programs(0) - 1)
    def _():
        ss = jnp.zeros((db, 1), F32)
        gated = []
        for k in range(SSD_INNER // LANES):
            cols = slice(k * LANES, (k + 1) * LANES)
            y = yT_ref[cols, :].T + dskip_ref[:, cols] * xs_ref[:, cols]
            zf = z_ref[:, cols].astype(F32)
            gk = y * (zf * _sigmoid(zf))
            gated.append(gk)
            ss = ss + jnp.sum(gk * gk, axis=-1, keepdims=True)
        scale = lax.rsqrt(ss * (1.0 / SSD_INNER) + NORM_EPS)
        for k in range(SSD_INNER // LANES):
            cols = slice(k * LANES, (k + 1) * LANES)
            y_ref[:, cols] = (gated[k] * scale * nw_ref[:, cols]).astype(BF16)


def _sample_ssd(xbc_s, z_s, dt_s, state_conv, state_ssm, conv_w, conv_b, dt_bias128, a_log128,
                dskip_row, ssd_nw):
    db = xbc_s.shape[0]
    assert db == LANES, "sample SSD kernel keeps the sequences on the lane axis"
    bb = SUBLANES
    _, _, eexp = _ssd_consts()
    k = jnp.arange(LANES)
    eexp1 = (jnp.where(k < SSD_HEADS, k, -1)[:, None] == (jnp.arange(SSD_INNER) // SSD_HEADDIM)[None, :]).astype(BF16)
    del eexp
    sconv2 = state_conv.reshape(db, (SSD_CONV - 1) * SSD_CONV_DIM)
    h3 = state_ssm.reshape(db, SSD_INNER, SSD_STATE)
    full = lambda a: _const_spec(a.shape)
    hblk = pl.BlockSpec((bb, SSD_INNER, SSD_STATE), lambda i: (i, 0, 0))
    y, sconv_new, h_new = pl.pallas_call(
        _sample_ssd_kernel,
        grid=(db // bb,),
        in_specs=[full(xbc_s), full(z_s), full(dt_s), full(sconv2), hblk,
                  full(conv_w), full(conv_b), full(dt_bias128), full(a_log128), full(dskip_row),
                  full(ssd_nw), full(eexp1)],
        out_specs=[pl.BlockSpec((db, SSD_INNER), lambda i: (0, 0)),
                   pl.BlockSpec(sconv2.shape, lambda i: (0, 0)), hblk],
        out_shape=[jax.ShapeDtypeStruct((db, SSD_INNER), BF16),
                   jax.ShapeDtypeStruct(sconv2.shape, F32),
                   jax.ShapeDtypeStruct(h3.shape, F32)],
        scratch_shapes=[pltpu.VMEM((SSD_INNER, db), BF16),
                        pltpu.VMEM((SSD_INNER, 2 * db), BF16),
                        pltpu.VMEM((SSD_GROUPS, db, SSD_STATE), F32),
                        pltpu.VMEM((SSD_GROUPS, SSD_STATE, db), F32),
                        pltpu.VMEM((db, SSD_INNER), F32),
                        pltpu.VMEM((SSD_INNER, db), F32)],
        compiler_params=_cparams(("arbitrary",)),
        name="sample_ssd",
    )(xbc_s, z_s, dt_s, sconv2, h3, conv_w, conv_b, dt_bias128, a_log128, dskip_row, ssd_nw, eexp1)
    return y, sconv_new, h_new


def _outproj_kernel(o0_ref, o1_ref, o2_ref, l0_ref, l1_ref, l2_ref, ssm_ref, x_ref,
                    e3_ref, wa_ref, ws_ref, nw_ref, y_ref, ob1_ref, ob2_ref, lb1_ref, lb2_ref):
    tm = x_ref.shape[1]
    hg = ATTN_HG

    def natural(ref, buf_ref):
        dil = ref.shape[1]
        if dil == 1:
            return ref[0, 0].astype(F32)
        chunks = []
        for c in range(ref.shape[3] // LANES):
            for r in range(dil):
                buf_ref[c, pl.ds(r, tm // dil, stride=dil), :] = ref[0, r, :, c * LANES:(c + 1) * LANES].astype(F32)
            chunks.append(buf_ref[c])
        return chunks[0] if len(chunks) == 1 else jnp.concatenate(chunks, axis=1)

    lse = natural(l0_ref, None) + natural(l1_ref, lb1_ref) + natural(l2_ref, lb2_ref)
    lane = lax.broadcasted_iota(jnp.int32, (tm, LANES), 1)
    used = lane < N_PATTERNS * hg

    def over_groups(v, op):
        r = op(op(v, pltpu.roll(v, LANES - hg, 1)), pltpu.roll(v, LANES - 2 * hg, 1))
        return jnp.where(lane < hg, r, jnp.where(lane < 2 * hg, pltpu.roll(r, hg, 1), pltpu.roll(r, 2 * hg, 1)))

    mx = over_groups(lse, jnp.maximum)
    e = jnp.where(used, jnp.exp(lse - mx), 0.0)
    alpha = jnp.where(used, e / over_groups(e, jnp.add), 0.0)
    a_hi, a_lo = _split2(alpha)
    packed = (a_hi.astype(F32) + pltpu.roll(a_lo.astype(F32), 4 * hg, 1)).astype(BF16)
    a_exp = jnp.dot(packed, e3_ref[...], preferred_element_type=F32)
    attn = (a_exp[:, :ATTN_OUT] * natural(o0_ref, None)
            + a_exp[:, ATTN_OUT:2 * ATTN_OUT] * natural(o1_ref, ob1_ref)
            + a_exp[:, 2 * ATTN_OUT:] * natural(o2_ref, ob2_ref))
    mix = (jnp.dot(attn.astype(BF16), wa_ref[...], preferred_element_type=F32)
           + jnp.dot(ssm_ref[0], ws_ref[...], preferred_element_type=F32))
    y_ref[0] = x_ref[0] + _rms_scale(mix) * nw_ref[...]


def _outproj(outs, lses, ssm, x3d, wa, ws, nw, jobs=()):
    b, s, _ = x3d.shape
    assert outs[0].shape[1] == 1
    tm = min(ROW_TILE, s)
    tpb = s // tm
    k = jnp.arange(LANES)
    slot = jnp.where(k % (4 * ATTN_HG) < N_PATTERNS * ATTN_HG, k % (4 * ATTN_HG), -1)
    slot = jnp.where(k < 8 * ATTN_HG, slot, -1)
    e3 = (slot[:, None] == (jnp.arange(N_PATTERNS * ATTN_OUT) // HEAD_DIM)[None, :]).astype(BF16)
    row = lambda w: pl.BlockSpec((1, tm, w), lambda i: (i // tpb, i % tpb, 0))
    grp = lambda a: pl.BlockSpec((1, a.shape[1], tm // a.shape[1], a.shape[3]), lambda i: (i // tpb, 0, i % tpb, 0))
    (y,), cache_res = _row_tile_call(
        _outproj_kernel, jobs,
        grid=(b * tpb,),
        in_specs=[grp(a) for a in outs] + [grp(a) for a in lses] + [row(SSD_INNER), row(D_MODEL),
                  _const_spec(e3.shape), _const_spec(wa.shape), _const_spec(ws.shape), _const_spec(nw.shape)],
        inputs=(*outs, *lses, ssm, x3d, e3, wa, ws, nw),
        out_specs=[row(D_MODEL)],
        out_shape=[jax.ShapeDtypeStruct((b, s, D_MODEL), F32)],
        scratch_shapes=[pltpu.VMEM((ATTN_OUT // LANES, tm, LANES), F32), pltpu.VMEM((ATTN_OUT // LANES, tm, LANES), F32),
                        pltpu.VMEM((1, tm, LANES), F32), pltpu.VMEM((1, tm, LANES), F32)],
        name="outproj")
    return y, cache_res


def _ffn_kernel(y_ref, nw1_ref, wg_ref, wu_ref, wo_ref, nw2_ref, out_ref, xn_ref, acc_ref):
    y = y_ref[...]
    xn_ref[...] = (_rms_scale(y) * nw1_ref[...]).astype(BF16)
    xn = xn_ref[...]
    for j in range(FFN_HIDDEN // FFN_CHUNK):
        cols = slice(j * FFN_CHUNK, (j + 1) * FFN_CHUNK)
        gate = jnp.dot(xn, wg_ref[:, cols], preferred_element_type=F32)
        up = jnp.dot(xn, wu_ref[:, cols], preferred_element_type=F32)
        h = (gate * _sigmoid(gate) * up).astype(BF16)
        part = jnp.dot(h, wo_ref[cols, :], preferred_element_type=F32)
        if j == 0:
            acc_ref[...] = part
        else:
            acc_ref[...] += part
    out_ref[...] = y + _rms_scale(acc_ref[...]) * nw2_ref[...]


def _ffn(y2d, nw1, wg, wu, wo, nw2, jobs=()):
    m = y2d.shape[0]
    tm = min(ROW_TILE, m)
    row = pl.BlockSpec((tm, D_MODEL), lambda i: (i, 0))
    (out,), cache_res = _row_tile_call(
        _ffn_kernel, jobs,
        grid=(m // tm,),
        in_specs=[row, _const_spec(nw1.shape), _const_spec(wg.shape), _const_spec(wu.shape),
                  _const_spec(wo.shape), _const_spec(nw2.shape)],
        inputs=(y2d, nw1, wg, wu, wo, nw2),
        out_specs=[row],
        out_shape=[jax.ShapeDtypeStruct((m, D_MODEL), F32)],
        scratch_shapes=[pltpu.VMEM((tm, D_MODEL), BF16), pltpu.VMEM((tm, D_MODEL), F32)],
        name="ffn")
    return out, cache_res


def _rope_tables(pos):
    half = HEAD_DIM // 2
    inv_freq = ROPE_THETA ** (-jnp.arange(half, dtype=F32) / half)
    ang = pos.astype(F32)[:, None] * inv_freq[None, :]
    cos, sin = jnp.cos(ang), jnp.sin(ang)
    reps = LANES // HEAD_DIM
    return (jnp.tile(jnp.concatenate([cos, cos], axis=1), (1, reps)),
            jnp.tile(jnp.concatenate([-sin, sin], axis=1), (1, reps)))


def _pad_lanes(v):
    return jnp.pad(v.astype(F32), (0, LANES - v.shape[0])).reshape(1, LANES)


def kernel(x_prompt, x_sample, cache_kv_w128, cache_kv_w512, cache_kv_w2048, state_conv, state_ssm,
           norm_mix_pre, norm_mix_post, norm_ffn_pre, norm_ffn_post, w_in, w_out, conv_w, conv_b,
           dt_bias, a_log, d_skip, ssd_norm_w, w_ffn_in, w_ffn_out):
    b, s, _ = x_prompt.shape
    db, ds, _ = x_sample.shape
    assert ds == 1 and norm_mix_pre.shape[0] == 1, "one layer, one sample token"
    assert s % ROW_TILE == 0 and s % (WIN_KEYS * ATTN_PATTERNS[-1][1]) == 0
    past_len = 8192

    w_in0 = w_in[0]
    o1, o2, o3 = ATTN_QKV, ATTN_QKV + SSD_INNER, ATTN_QKV + SSD_INNER + SSD_CONV_DIM
    wqkv = w_in0[:, :o1].astype(BF16)
    wz = w_in0[:, o1:o2].astype(BF16)
    wxbc = w_in0[:, o2:o3].astype(BF16)
    wdt = jnp.pad(w_in0[:, o3:], ((0, 0), (0, LANES - SSD_HEADS))).astype(BF16)
    wa = w_out[0][:ATTN_OUT].astype(BF16)
    ws = w_out[0][ATTN_OUT:].astype(BF16)
    wg = w_ffn_in[0][:, :FFN_HIDDEN].astype(BF16)
    wu = w_ffn_in[0][:, FFN_HIDDEN:].astype(BF16)
    wo = w_ffn_out[0].astype(BF16)
    nw_mix_pre, nw_mix_post = norm_mix_pre.reshape(1, -1), norm_mix_post.reshape(1, -1)
    nw_ffn_pre, nw_ffn_post = norm_ffn_pre.reshape(1, -1), norm_ffn_post.reshape(1, -1)
    ssd_nw = ssd_norm_w.reshape(1, -1)
    cw, cb = conv_w[0], conv_b.reshape(1, -1)
    dtb, alog = _pad_lanes(dt_bias[0]), _pad_lanes(a_log[0])
    dskip_row = jnp.repeat(d_skip[0].astype(F32), SSD_HEADDIM).reshape(1, SSD_INNER)
    caches = (cache_kv_w128[0], cache_kv_w512[0], cache_kv_w2048[0])

    xs3 = x_sample.reshape(1, db, D_MODEL)
    cos_s, sin_s = _rope_tables(jnp.full((db,), past_len))
    (*qkv_s, z_s, xbc_s, dt_s), _ = _inproj(xs3, nw_mix_pre, cos_s, sin_s, wqkv, wz, wxbc, wdt, (1,) * N_PATTERNS)
    qkv_s = [q.reshape(db, 3 * ATTN_OUT) for q in qkv_s]

    big = N_PATTERNS - 1
    ride_along = b * (s // min(ROW_TILE, s)) == db
    head_slices = {"inproj": (0, 2), "ffn": (2, 2), "outproj": (4, 4)}
    whole = {"inproj": 0, "ffn": 1}
    if ride_along:
        cache_in = [_sample_cache_inputs(qkv_s[g], caches[g]) for g in range(N_PATTERNS)]

    def jobs(key, partial):
        if not ride_along:
            return ()
        c_big, q4_big = cache_in[big]
        res = [(q4_big, c_big, partial, ATTN_PATTERNS[big][1]) + head_slices[key]]
        if key in whole:
            c_g, q4_g = cache_in[whole[key]]
            res.append((q4_g, c_g, None, ATTN_PATTERNS[whole[key]][1], 0, ATTN_HG))
        return res

    dils = tuple(d for _, d in ATTN_PATTERNS)
    cos_p, sin_p = _rope_tables(jnp.arange(s))
    (*qkv_groups, z, xbc, dt_raw), ride0 = _inproj(x_prompt, nw_mix_pre, cos_p, sin_p, wqkv, wz, wxbc, wdt, dils,
                                                    jobs("inproj", None))
    outs, lses = zip(*[_prompt_attention(qkv_groups[g], g) for g in range(N_PATTERNS)])
    ssm, p_conv, h_t = _prompt_ssd(xbc, z, dt_raw, cw, cb, dtb, alog, dskip_row, ssd_nw, b, s)
    y1, ride1 = _outproj(outs, lses, ssm, x_prompt, wa, ws, nw_mix_post,
                         jobs("outproj", ride0[0][0] if ride_along else None))
    y_prompt, ride2 = _ffn(y1.reshape(b * s, D_MODEL), nw_ffn_pre, wg, wu, wo, nw_ffn_post,
                           jobs("ffn", ride1[0][0] if ride_along else None))
    y_prompt = y_prompt.reshape(b, s, D_MODEL)

    p_kv = [_prompt_kv_tail(qkv_groups[g], window, s) for g, (window, _) in enumerate(ATTN_PATTERNS)]
    p_ssm = h_t.reshape(b, SSD_STATE, SSD_HEADS, SSD_HEADDIM).transpose(0, 2, 3, 1)[None]

    s_kv, outs_s, lses_s = [], [], []
    for g in range(N_PATTERNS):
        if g == big and ride_along:
            rows = jnp.concatenate([r[0][1][:, 0, :head_slices[key][1]] for key, r in
                                    (("inproj", ride0), ("ffn", ride2), ("outproj", ride1))], axis=1)
            kv_new, o_g, lse_g = _sample_cache_outputs(ride2[0][0], rows)
        elif ride_along:
            co, rows = {"inproj": ride0, "ffn": ride2}[[k for k, v in whole.items() if v == g][0]][1]
            kv_new, o_g, lse_g = _sample_cache_outputs(co, rows[:, 0])
        else:
            kv_new, o_g, lse_g = _sample_cache_step(qkv_s[g], caches[g], g)
        s_kv.append(kv_new)
        outs_s.append(o_g.astype(BF16).reshape(1, 1, db, ATTN_OUT))
        lse_g = jnp.pad(lse_g, ((0, 0), (g * ATTN_HG, LANES - (g + 1) * ATTN_HG)))
        lses_s.append(lse_g.reshape(1, 1, db, LANES))
    ssm_s, sconv_new, h_new = _sample_ssd(xbc_s[0], z_s[0], dt_s[0], state_conv[0], state_ssm[0], cw, cb, dtb, alog,
                                          dskip_row, ssd_nw)
    y1_s, _ = _outproj(outs_s, lses_s, ssm_s[None], xs3, wa, ws, nw_mix_post)
    y_sample = _ffn(y1_s[0], nw_ffn_pre, wg, wu, wo, nw_ffn_post)[0].reshape(db, 1, D_MODEL)

    s_conv = sconv_new.reshape(1, db, SSD_CONV - 1, SSD_CONV_DIM)
    s_ssm = h_new.reshape(1, db, SSD_HEADS, SSD_HEADDIM, SSD_STATE)
    return (y_prompt, y_sample, p_kv[0], p_kv[1], p_kv[2], p_conv[None], p_ssm,
            s_kv[0], s_kv[1], s_kv[2], s_conv, s_ssm)
```

```python
import functools

import jax
import jax.numpy as jnp
from jax import lax
from jax.experimental import pallas as pl
from jax.experimental.pallas import tpu as pltpu

F32 = jnp.float32
BF16 = jnp.bfloat16

D_MODEL = 1024
HEAD_DIM = 64
ATTN_PATTERNS = ((128, 1), (512, 4), (2048, 16))
N_PATTERNS = len(ATTN_PATTERNS)
ATTN_HG = 8
WIN_KEYS = 128
ATTN_OUT = ATTN_HG * HEAD_DIM
ATTN_QKV = N_PATTERNS * 3 * ATTN_OUT
ROPE_THETA = 10000.0
ATTN_SCALE = HEAD_DIM ** -0.5
NEG_INF = -1e30
SSD_INNER = D_MODEL
SSD_HEADDIM = 64
SSD_HEADS = SSD_INNER // SSD_HEADDIM
SSD_GROUPS = 2
SSD_HPG = SSD_HEADS // SSD_GROUPS
SSD_STATE = 128
SSD_CONV = 4
SSD_CONV_DIM = SSD_INNER + 2 * SSD_GROUPS * SSD_STATE
FFN_HIDDEN = 2816
NORM_EPS = 1e-6

LANES = 128
SUBLANES = 8
ROW_TILE = 512
SSD_T = 128
SSD_CHUNKS_PER_STEP = 4
FFN_CHUNK = 256
VMEM_LIMIT = 56 * 1024 * 1024
CACHE_BLOCK_BYTES = 4 * 1024 * 1024


def _cparams(sem):
    return pltpu.CompilerParams(dimension_semantics=sem, vmem_limit_bytes=VMEM_LIMIT)


def _const_spec(shape):
    nd = len(shape)
    return pl.BlockSpec(shape, lambda *_: (0,) * nd, pipeline_mode=pl.Buffered(1))


def _split2(x):
    hi = x.astype(BF16)
    lo = (x - hi.astype(F32)).astype(BF16)
    return hi, lo


def _split3(x):
    hi = x.astype(BF16)
    r = x - hi.astype(F32)
    mid = r.astype(BF16)
    lo = (r - mid.astype(F32)).astype(BF16)
    return hi, mid, lo


def _sigmoid(x):
    return 1.0 / (1.0 + jnp.exp(-x))


def _softplus(x):
    return jnp.maximum(x, 0.0) + jnp.log(1.0 + jnp.exp(-jnp.abs(x)))


def _rms_scale(x):
    return x * lax.rsqrt(jnp.mean(x * x, axis=-1, keepdims=True) + NORM_EPS)


def _with_cache_jobs(kernel_fn, n_in, n_out, n_scratch, job_params):
    def wrapped(*refs):
        ins, rest = refs[:n_in], refs[n_in:]
        c_ins = []
        for n_cache_in, _, _, _ in job_params:
            c_ins.append(rest[:n_cache_in])
            rest = rest[n_cache_in:]
        outs, rest = rest[:n_out], rest[n_out:]
        c_outs, rest = rest[:2 * len(job_params)], rest[2 * len(job_params):]
        scratch, col_refs = rest[:n_scratch], rest[n_scratch:]
        for j, (_, dil, head_lo, hv) in enumerate(job_params):
            _sample_cache_body(c_ins[j][0], c_ins[j][1], c_outs[2 * j], c_outs[2 * j + 1], col_refs[j], head_lo, dil, hv)
        kernel_fn(*ins, *outs, *scratch)
    return wrapped


def _row_tile_call(kernel_fn, jobs, *, grid, in_specs, inputs, out_specs, out_shape, scratch_shapes, name):
    in_specs, inputs = list(in_specs), list(inputs)
    out_specs, out_shape, scratch_shapes = list(out_specs), list(out_shape), list(scratch_shapes)
    n_in, n_out, n_scratch = len(in_specs), len(out_specs), len(scratch_shapes)
    job_params, cache_blocks, aliases = [], [], {}
    for q4, cache_t, partial, dil, head_lo, n_heads in jobs:
        db, _, _, hd, wb = cache_t.shape
        assert grid == (db,) and head_lo % n_heads == 0
        cblk = pl.BlockSpec((1, 2, n_heads, hd, wb), lambda i, blk=head_lo // n_heads: (i, 0, blk, 0, 0))
        cache_blocks.append(cblk)
        in_specs += [pl.BlockSpec((1, 3, ATTN_HG, LANES), lambda i: (i, 0, 0, 0)), cblk]
        inputs += [q4, cache_t]
        if partial is not None:
            aliases[len(in_specs)] = n_out + 2 * len(job_params)
            in_specs.append(pl.BlockSpec(memory_space=pl.ANY))
            inputs.append(partial)
        hv = n_heads if n_heads == ATTN_HG and 2 * n_heads * hd * wb * 4 <= CACHE_BLOCK_BYTES // 2 else 1
        job_params.append((2 if partial is None else 3, dil, head_lo, hv))
    for (_, cache_t, *_), cblk in zip(jobs, cache_blocks):
        out_specs += [cblk, pl.BlockSpec((1, 1, SUBLANES, LANES), lambda i: (i, 0, 0, 0))]
        out_shape += [jax.ShapeDtypeStruct(cache_t.shape, F32),
                      jax.ShapeDtypeStruct((cache_t.shape[0], 1, SUBLANES, LANES), F32)]
        scratch_shapes.append(pltpu.VMEM((3, ATTN_HG, cache_t.shape[3], 1), F32))
    body = _with_cache_jobs(kernel_fn, n_in, n_out, n_scratch, job_params) if jobs else kernel_fn
    res = pl.pallas_call(body, grid=grid, in_specs=in_specs, out_specs=out_specs, out_shape=out_shape,
                         scratch_shapes=scratch_shapes, input_output_aliases=aliases,
                         compiler_params=_cparams(("arbitrary",)), name=name)(*inputs)
    return res[:n_out], [(res[n_out + 2 * j], res[n_out + 2 * j + 1]) for j in range(len(jobs))]


def _inproj_kernel(x_ref, nw_ref, cos_ref, sin_ref, wqkv_ref, wz_ref, wxbc_ref, wdt_ref,
                   qkv0_ref, qkv1_ref, qkv2_ref, z_ref, xbc_ref, dt_ref, xn_ref, perm_ref, perm4_ref, *, dils):
    tm = x_ref.shape[1]
    xn_ref[...] = (_rms_scale(x_ref[0]) * nw_ref[...]).astype(BF16)
    xn = xn_ref[...]
    cos = cos_ref[...]
    sin = sin_ref[...]
    lane = lax.broadcasted_iota(jnp.int32, (tm, LANES), 1)
    first_half = (lane % HEAD_DIM) < (HEAD_DIM // 2)
    group_refs = (qkv0_ref, qkv1_ref, qkv2_ref)
    for j in range(ATTN_QKV // ATTN_OUT):
        group, part = divmod(j, 3)
        dil = dils[group]
        out_ref = group_refs[group]
        acc = jnp.dot(xn, wqkv_ref[:, j * ATTN_OUT:(j + 1) * ATTN_OUT], preferred_element_type=F32)
        for c in range(ATTN_OUT // LANES):
            a = acc[:, c * LANES:(c + 1) * LANES]
            if part != 2:
                partner = jnp.where(first_half, pltpu.roll(a, LANES - HEAD_DIM // 2, 1),
                                    pltpu.roll(a, HEAD_DIM // 2, 1))
                a = a * cos + partner * sin
                if part == 0:
                    a = a * ATTN_SCALE
            cols = slice(part * ATTN_OUT + c * LANES, part * ATTN_OUT + (c + 1) * LANES)
            if dil == 1:
                out_ref[0, 0, :, cols] = a.astype(BF16)
            else:
                perm_ref[...] = a
                if dil <= 4:
                    for r in range(dil):
                        out_ref[0, r, :, cols] = perm_ref[pl.ds(r, tm // dil, stride=dil), :].astype(BF16)
                else:
                    for r_lo in range(4):
                        perm4_ref[r_lo] = perm_ref[pl.ds(r_lo, tm // 4, stride=4), :]
                    for r_lo in range(4):
                        for r_hi in range(dil // 4):
                            out_ref[0, 4 * r_hi + r_lo, :, cols] = (
                                perm4_ref[r_lo, pl.ds(r_hi, tm // dil, stride=dil // 4), :].astype(BF16))
    z_ref[0] = jnp.dot(xn, wz_ref[...], preferred_element_type=F32).astype(BF16)
    xbc_ref[0] = jnp.dot(xn, wxbc_ref[...], preferred_element_type=F32).astype(BF16)
    dt_ref[0] = jnp.dot(xn, wdt_ref[...], preferred_element_type=F32)


def _inproj(x3d, nw, cos_t, sin_t, wqkv, wz, wxbc, wdt, dils, jobs=()):
    b, s, _ = x3d.shape
    tm = min(ROW_TILE, s)
    tpb = s // tm
    assert cos_t.shape[0] == s
    row = lambda w: pl.BlockSpec((1, tm, w), lambda i: (i // tpb, i % tpb, 0))
    tab = pl.BlockSpec((tm, LANES), lambda i: (i % tpb, 0))
    grp = lambda d: pl.BlockSpec((1, d, tm // d, 3 * ATTN_OUT), lambda i: (i // tpb, 0, i % tpb, 0))
    return _row_tile_call(
        functools.partial(_inproj_kernel, dils=dils), jobs,
        grid=(b * tpb,),
        in_specs=[row(D_MODEL), _const_spec((1, D_MODEL)), tab, tab,
                  _const_spec(wqkv.shape), _const_spec(wz.shape), _const_spec(wxbc.shape),
                  _const_spec(wdt.shape)],
        inputs=(x3d, nw, cos_t, sin_t, wqkv, wz, wxbc, wdt),
        out_specs=[grp(d) for d in dils] + [row(SSD_INNER), row(SSD_CONV_DIM), row(LANES)],
        out_shape=[jax.ShapeDtypeStruct((b, d, s // d, 3 * ATTN_OUT), BF16) for d in dils]
                  + [jax.ShapeDtypeStruct((b, s, SSD_INNER), BF16),
                     jax.ShapeDtypeStruct((b, s, SSD_CONV_DIM), BF16),
                     jax.ShapeDtypeStruct((b, s, LANES), F32)],
        scratch_shapes=[pltpu.VMEM((tm, D_MODEL), BF16), pltpu.VMEM((tm, LANES), F32),
                        pltpu.VMEM((4, tm // 4, LANES), F32)],
        name="inproj")


def _attn_kernel(q_ref, kc_ref, vc_ref, kp_ref, vp_ref, o_ref, lse_ref, k_scr, v_scr, *, group):
    cq = q_ref.shape[0]
    w = WIN_KEYS
    n = pl.program_id(2)
    k_scr[0:w] = kp_ref[...]
    k_scr[w:] = kc_ref[...]
    v_scr[0:w] = vp_ref[...]
    v_scr[w:] = vc_ref[...]
    lane = lax.broadcasted_iota(jnp.int32, (w, LANES), 1)
    lo = lane < HEAD_DIM
    qi = lax.broadcasted_iota(jnp.int32, (w, 2 * w), 0)
    ki = lax.broadcasted_iota(jnp.int32, (w, 2 * w), 1)
    band = ((ki < w) & (ki >= qi)) | ((ki >= w) & ((ki - w) <= qi))
    for i in range(cq // w):
        if i == 0:
            kmin = jnp.where(n > 0, 0, w)
            valid = band & (ki >= kmin)
        else:
            valid = band
        lse_tile = jnp.zeros((w, LANES), F32)
        for pair in range(ATTN_OUT // LANES):
            cols = slice(pair * LANES, (pair + 1) * LANES)
            qp = q_ref[i * w:(i + 1) * w, cols]
            kk = k_scr[i * w:(i + 2) * w, cols]
            vv = v_scr[i * w:(i + 2) * w, cols]
            halves = []
            for half in range(2):
                qh = jnp.where(lo if half == 0 else jnp.logical_not(lo), qp, jnp.zeros_like(qp))
                s = lax.dot_general(qh, kk, (((1,), (1,)), ((), ())), preferred_element_type=F32)
                s = jnp.where(valid, s, NEG_INF)
                m = jnp.max(s, axis=-1, keepdims=True)
                p = jnp.exp(s - m)
                den = jnp.sum(p, axis=-1, keepdims=True)
                pv = jnp.dot(p.astype(BF16), vv, preferred_element_type=F32)
                halves.append(pv / den)
                head = 2 * pair + half
                lse_tile = jnp.where(lane == group * ATTN_HG + head, m + jnp.log(den), lse_tile)
            o_ref[i * w:(i + 1) * w, cols] = jnp.where(lo, halves[0], halves[1]).astype(BF16)
        lse_ref[i * w:(i + 1) * w, :] = lse_tile


def _prompt_attention(qkv_g, group):
    b, dil, sub_len, _ = qkv_g.shape
    cq = min(ROW_TILE, sub_len)
    blk_per_chunk = cq // WIN_KEYS
    cur = lambda t: pl.BlockSpec((None, None, cq, ATTN_OUT), lambda bi, r, n: (bi, r, n, t))
    prev = lambda t: pl.BlockSpec(
        (None, None, WIN_KEYS, ATTN_OUT),
        lambda bi, r, n: (bi, r, jnp.maximum(n * blk_per_chunk - 1, 0), t))
    out = lambda wdt: pl.BlockSpec((None, None, cq, wdt), lambda bi, r, n: (bi, r, n, 0))
    return pl.pallas_call(
        functools.partial(_attn_kernel, group=group),
        grid=(b, dil, sub_len // cq),
        in_specs=[cur(0), cur(1), cur(2), prev(1), prev(2)],
        out_specs=[out(ATTN_OUT), out(LANES)],
        out_shape=[jax.ShapeDtypeStruct((b, dil, sub_len, ATTN_OUT), BF16),
                   jax.ShapeDtypeStruct((b, dil, sub_len, LANES), F32)],
        scratch_shapes=[pltpu.VMEM((cq + WIN_KEYS, ATTN_OUT), BF16),
                        pltpu.VMEM((cq + WIN_KEYS, ATTN_OUT), BF16)],
        compiler_params=_cparams(("arbitrary", "arbitrary", "arbitrary")),
        name=f"prompt_attn_g{group}",
    )(qkv_g, qkv_g, qkv_g, qkv_g, qkv_g)


def _pack3_lanes(x):
    hi, mid, lo = _split3(x)
    packed = hi.astype(F32) + pltpu.roll(mid.astype(F32), SSD_HEADS, 1) + pltpu.roll(lo.astype(F32), 2 * SSD_HEADS, 1)
    return packed.astype(BF16)


def _ssd_kernel(xbc_ref, z_ref, dtr_ref, convw_ref, convb_ref, dtb_ref, alog_ref, dskip_ref, nw_ref,
                tri3_ref, rexp_ref, eexp_ref, shift_ref,
                y_ref, pconv_ref, hT_out_ref,
                x2_ref, hT_ref, csb_ref, ybuf_ref):
    t = SSD_T
    n_sub = xbc_ref.shape[1] // t
    c = pl.program_id(1)
    last = pl.num_programs(1) - 1
    taps = SSD_CONV

    @pl.when(c == 0)
    def _():
        x2_ref[0:t] = jnp.zeros((t, SSD_CONV_DIM), BF16)
        hT_ref[...] = jnp.zeros_like(hT_ref)

    lane = lax.broadcasted_iota(jnp.int32, (t, LANES), 1)
    ti = lax.broadcasted_iota(jnp.int32, (t, t), 0)
    si = lax.broadcasted_iota(jnp.int32, (t, t), 1)
    tri = si <= ti
    lo = lane < SSD_HEADDIM
    gn = SSD_GROUPS * SSD_STATE
    pairs_per_group = SSD_HPG // 2
    x2_ref[t:] = xbc_ref[0]

    for sub in range(n_sub):
        rows = slice(sub * t, (sub + 1) * t)
        cur_f = xbc_ref[0, rows].astype(F32)
        acc = convb_ref[...] + convw_ref[taps - 1:taps, :] * cur_f
        for j in range(taps - 1):
            shifted = jnp.dot(shift_ref[j], x2_ref[sub * t:(sub + 2) * t], preferred_element_type=F32)
            acc = acc + convw_ref[j:j + 1, :] * shifted

        if sub == n_sub - 1:
            @pl.when(c == last)
            def _():
                pconv_ref[0] = cur_f[t - (taps - 1):, :]

        xc = acc * _sigmoid(acc)
        xs = xc[:, :SSD_INNER]
        xs_b = xs.astype(BF16)

        dt = jnp.where(lane < SSD_HEADS, _softplus(dtr_ref[0, rows] + dtb_ref[...]), 0.0)
        a = dt * (-jnp.exp(alog_ref[...]))
        a3 = jnp.concatenate(_split3(a), axis=0)
        cs = jnp.dot(tri3_ref[...], a3, preferred_element_type=F32)
        cs_last = cs[t - 1:t, :]
        wl = jnp.exp(cs_last - cs) * dt
        csT = cs.T
        dtT = dt.T
        wlT = wl.T
        csb_ref[sub] = jnp.dot(_pack3_lanes(cs), rexp_ref[...], preferred_element_type=F32)
        cs_last8 = jnp.broadcast_to(cs_last, (SUBLANES, LANES))
        dec = jnp.exp(jnp.dot(_pack3_lanes(cs_last8), eexp_ref[...], preferred_element_type=F32)[0:1, :])

        g_mat, bT, yoff = [], [], []
        for g in range(SSD_GROUPS):
            bm = xc[:, SSD_INNER + g * SSD_STATE:SSD_INNER + (g + 1) * SSD_STATE]
            cm = xc[:, SSD_INNER + gn + g * SSD_STATE:SSD_INNER + gn + (g + 1) * SSD_STATE].astype(BF16)
            g_mat.append(lax.dot_general(cm, bm.astype(BF16), (((1,), (1,)), ((), ())),
                                         preferred_element_type=F32))
            bT.append(bm.T)
            cols = slice(g * SSD_HPG * SSD_HEADDIM, (g + 1) * SSD_HPG * SSD_HEADDIM)
            yoff.append(jnp.dot(cm, hT_ref[:, cols].astype(BF16), preferred_element_type=F32))

        ss = jnp.zeros((t, 1), F32)
        for pair in range(SSD_HEADS // 2):
            g = pair // pairs_per_group
            cols = slice(pair * LANES, (pair + 1) * LANES)
            w_blocks, b_blocks = [], []
            for h in (2 * pair, 2 * pair + 1):
                seg = csb_ref[sub, :, h * LANES:(h + 1) * LANES] - csT[h:h + 1, :]
                seg = jnp.where(tri, seg, NEG_INF)
                w_blocks.append((g_mat[g] * jnp.exp(seg) * dtT[h:h + 1, :]).astype(BF16))
                b_blocks.append((bT[g] * wlT[h:h + 1, :]).astype(BF16))
            lhs = jnp.concatenate([jnp.concatenate(w_blocks, axis=1), jnp.concatenate(b_blocks, axis=1)], axis=0)
            xp = xs_b[:, cols]
            zero = jnp.zeros_like(xp)
            xbd = jnp.concatenate([jnp.where(lo, xp, zero), jnp.where(lo, zero, xp)], axis=0)
            res = jnp.dot(lhs, xbd, preferred_element_type=F32)
            ecs = jnp.exp(jnp.where(lo, csb_ref[sub, :, 2 * pair * LANES:(2 * pair + 1) * LANES],
                                    csb_ref[sub, :, (2 * pair + 1) * LANES:(2 * pair + 2) * LANES]))
            gcol = (pair % pairs_per_group) * LANES
            y = res[:t] + yoff[g][:, gcol:gcol + LANES] * ecs + dskip_ref[:, cols] * xs[:, cols]
            hT_ref[:, cols] = hT_ref[:, cols] * dec[:, cols] + res[t:]
            zf = z_ref[0, rows, cols].astype(F32)
            gated = y * (zf * _sigmoid(zf))
            ybuf_ref[sub, :, cols] = gated
            ss = ss + jnp.sum(gated * gated, axis=-1, keepdims=True)

        y_ref[0, rows] = (ybuf_ref[sub] * lax.rsqrt(ss * (1.0 / SSD_INNER) + NORM_EPS) * nw_ref[...]).astype(BF16)

    x2_ref[0:t] = xbc_ref[0, (n_sub - 1) * t:]

    @pl.when(c == last)
    def _():
        hT_out_ref[0] = hT_ref[...]


def _ssd_consts():
    t = SSD_T
    tri = (jnp.arange(t)[:, None] >= jnp.arange(t)[None, :]).astype(BF16)
    tri3 = jnp.concatenate([tri, tri, tri], axis=1)
    k = jnp.arange(LANES)
    piece_head = jnp.where(k < 3 * SSD_HEADS, k % SSD_HEADS, -1)
    rexp = (piece_head[:, None] == (jnp.arange(SSD_HEADS * LANES) // LANES)[None, :]).astype(BF16)
    eexp = (piece_head[:, None] == (jnp.arange(SSD_INNER) // SSD_HEADDIM)[None, :]).astype(BF16)
    return tri3, rexp, eexp


def _prompt_ssd(xbc, z, dt_raw, conv_w, conv_b, dt_bias128, a_log128, dskip_row, ssd_nw, b, s):
    t = SSD_T
    tri3, rexp, eexp = _ssd_consts()
    back = (SSD_CONV - 1 - jnp.arange(SSD_CONV - 1))[:, None, None]
    shift = (jnp.arange(2 * t)[None, None, :] == t + jnp.arange(t)[None, :, None] - back).astype(BF16)
    n_sub = SSD_CHUNKS_PER_STEP if s % (SSD_CHUNKS_PER_STEP * t) == 0 else 1
    tok = lambda w: pl.BlockSpec((1, n_sub * t, w), lambda bi, c: (bi, c, 0))
    per_b = lambda *shape: pl.BlockSpec((1,) + shape, lambda bi, c: (bi,) + (0,) * len(shape))
    return pl.pallas_call(
        _ssd_kernel,
        grid=(b, s // (n_sub * t)),
        in_specs=[tok(SSD_CONV_DIM), tok(SSD_INNER), tok(LANES),
                  _const_spec(conv_w.shape), _const_spec(conv_b.shape), _const_spec(dt_bias128.shape),
                  _const_spec(a_log128.shape), _const_spec(dskip_row.shape), _const_spec(ssd_nw.shape),
                  _const_spec(tri3.shape), _const_spec(rexp.shape), _const_spec(eexp.shape),
                  _const_spec(shift.shape)],
        out_specs=[tok(SSD_INNER), per_b(SSD_CONV - 1, SSD_CONV_DIM), per_b(SSD_STATE, SSD_INNER)],
        out_shape=[jax.ShapeDtypeStruct((b, s, SSD_INNER), BF16),
                   jax.ShapeDtypeStruct((b, SSD_CONV - 1, SSD_CONV_DIM), F32),
                   jax.ShapeDtypeStruct((b, SSD_STATE, SSD_INNER), F32)],
        scratch_shapes=[pltpu.VMEM(((n_sub + 1) * t, SSD_CONV_DIM), BF16),
                        pltpu.VMEM((SSD_STATE, SSD_INNER), F32),
                        pltpu.VMEM((n_sub, t, SSD_HEADS * LANES), F32),
                        pltpu.VMEM((n_sub, t, SSD_INNER), F32)],
        compiler_params=_cparams(("arbitrary", "arbitrary")),
        name="prompt_ssd",
    )(xbc.reshape(b, s, SSD_CONV_DIM), z.reshape(b, s, SSD_INNER), dt_raw.reshape(b, s, LANES),
      conv_w, conv_b, dt_bias128, a_log128, dskip_row, ssd_nw, tri3, rexp, eexp, shift)


def _pkv_kernel(k_ref, v_ref, out_ref, buf_ref):
    _, dil, rows, _ = k_ref.shape
    tb = dil * rows
    for part, ref in enumerate((k_ref, v_ref)):
        for c in range(ATTN_OUT // LANES):
            cols = slice(c * LANES, (c + 1) * LANES)
            if dil == 1:
                nat = ref[0, 0, :, cols].astype(F32)
            else:
                for r in range(dil):
                    buf_ref[pl.ds(r, rows, stride=dil), :] = ref[0, r, :, cols].astype(F32)
                nat = buf_ref[...]
            base = part * ATTN_OUT + c * LANES
            for tblk in range(tb // LANES):
                out_ref[0, base:base + LANES, tblk * LANES:(tblk + 1) * LANES] = nat[tblk * LANES:(tblk + 1) * LANES, :].T


def _prompt_kv_tail(qkv_g, window, s):
    b, dil, _, _ = qkv_g.shape
    wlen = min(window, s)
    tb = min(ROW_TILE, wlen)
    assert (s - wlen) % tb == 0 and tb % (dil * 2 * SUBLANES) == 0
    first = (s - wlen) // tb
    blk = lambda t: pl.BlockSpec((1, dil, tb // dil, ATTN_OUT), lambda bi, n: (bi, 0, first + n, t))
    out = pl.pallas_call(
        _pkv_kernel,
        grid=(b, wlen // tb),
        in_specs=[blk(1), blk(2)],
        out_specs=pl.BlockSpec((1, 2 * ATTN_OUT, tb), lambda bi, n: (bi, 0, n)),
        out_shape=jax.ShapeDtypeStruct((b, 2 * ATTN_OUT, wlen), F32),
        scratch_shapes=[pltpu.VMEM((tb, LANES), F32)],
        compiler_params=_cparams(("arbitrary", "arbitrary")),
        name=f"prompt_kv_tail_w{window}",
    )(qkv_g, qkv_g)
    return jnp.transpose(out.reshape(1, b, 2, ATTN_HG, HEAD_DIM, wlen), (0, 1, 5, 2, 3, 4))


def _sample_cache_kernel(q_ref, c_ref, co_ref, o_ref, col_ref, *, dil, hv):
    _sample_cache_body(q_ref, c_ref, co_ref, o_ref, col_ref, pl.program_id(1) * c_ref.shape[2], dil, hv)


def _sample_cache_body(q_ref, c_ref, co_ref, o_ref, col_ref, head0, dil, hv):
    bb, _, hb, hd, wb = c_ref.shape
    pos = lax.broadcasted_iota(jnp.int32, (1, 1, wb), 2)
    in_window = (pos & (dil - 1)) == 0
    is_last = lax.broadcasted_iota(jnp.int32, (1, hd, wb), 2) == wb - 1
    lane = lax.broadcasted_iota(jnp.int32, (hd, LANES), 1)
    lane_row = lax.broadcasted_iota(jnp.int32, (1, LANES), 1)
    for bi in range(bb):
        for part in range(3):
            cols = q_ref[bi, part].T
            for h in range(ATTN_HG):
                col_ref[part, h] = cols[0:hd, h:h + 1]
        o_cols = jnp.zeros((hd, LANES), F32)
        lse_row = jnp.zeros((1, LANES), F32)
        for h in range(0, hb, hv):
            q = col_ref[0, pl.ds(head0 + h, hv)]
            k_new = col_ref[1, pl.ds(head0 + h, hv)]
            v_new = col_ref[2, pl.ds(head0 + h, hv)]
            keys = c_ref[bi, 0, h:h + hv]
            vals = c_ref[bi, 1, h:h + hv]
            sc = jnp.where(in_window, jnp.sum(keys * q, axis=1, keepdims=True), NEG_INF)
            sc_new = jnp.sum(k_new * q, axis=1, keepdims=True)
            m = jnp.maximum(jnp.max(sc, axis=2, keepdims=True), sc_new)
            p = jnp.exp(sc - m)
            p_new = jnp.exp(sc_new - m)
            den = jnp.sum(p, axis=2, keepdims=True) + p_new
            out = (jnp.sum(vals * p, axis=2, keepdims=True) + v_new * p_new) / den
            lse = m + jnp.log(den)
            for t in range(hv):
                o_cols = jnp.where(lane == h + t, out[t], o_cols)
                lse_row = jnp.where(lane_row == h + t, lse[t], lse_row)
            co_ref[bi, 0, h:h + hv] = jnp.where(is_last, k_new, pltpu.roll(keys, wb - 1, 2))
            co_ref[bi, 1, h:h + hv] = jnp.where(is_last, v_new, pltpu.roll(vals, wb - 1, 2))
        tile = jnp.concatenate([o_cols, jnp.broadcast_to(lse_row, (SUBLANES, LANES)),
                                jnp.zeros((LANES - hd - SUBLANES, LANES), F32)], axis=0)
        o_ref[bi, 0] = tile.T[0:SUBLANES, :]


def _sample_cache_step(qkv_g, cache, group):
    db = qkv_g.shape[0]
    window, dil = ATTN_PATTERNS[group]
    wb = cache.shape[1]
    assert wb == window and dil & (dil - 1) == 0, "sample path expects a full window of cached rows"
    c_t, q4 = _sample_cache_inputs(qkv_g, cache)
    row_bytes = 2 * HEAD_DIM * wb * 4
    hb = max(1, min(ATTN_HG, CACHE_BLOCK_BYTES // row_bytes))
    bb = max(1, min(SUBLANES, CACHE_BLOCK_BYTES // (row_bytes * ATTN_HG)))
    hv = hb if hb == ATTN_HG else 1
    n_hblk = ATTN_HG // hb
    cblk = pl.BlockSpec((bb, 2, hb, HEAD_DIM, wb), lambda i, j: (i, 0, j, 0, 0))
    co, o = pl.pallas_call(
        functools.partial(_sample_cache_kernel, dil=dil, hv=hv),
        grid=(db // bb, n_hblk),
        in_specs=[pl.BlockSpec((bb, 3, ATTN_HG, LANES), lambda i, j: (i, 0, 0, 0)), cblk],
        out_specs=[cblk, pl.BlockSpec((bb, 1, SUBLANES, LANES), lambda i, j: (i, j, 0, 0))],
        out_shape=[jax.ShapeDtypeStruct(c_t.shape, F32),
                   jax.ShapeDtypeStruct((db, n_hblk, SUBLANES, LANES), F32)],
        scratch_shapes=[pltpu.VMEM((3, ATTN_HG, HEAD_DIM, 1), F32)],
        compiler_params=_cparams(("arbitrary", "arbitrary")),
        name=f"sample_cache_g{group}",
    )(q4, c_t)
    return _sample_cache_outputs(co, o[:, :, :hb].reshape(db, ATTN_HG, LANES))


def _sample_cache_inputs(qkv_g, cache):
    db = qkv_g.shape[0]
    c_t = jnp.transpose(cache, (0, 2, 3, 4, 1))
    q4 = jnp.pad(qkv_g.astype(F32).reshape(db, 3, ATTN_HG, HEAD_DIM), ((0, 0), (0, 0), (0, 0), (0, LANES - HEAD_DIM)))
    return c_t, q4


def _sample_cache_outputs(co, rows):
    db = rows.shape[0]
    return jnp.transpose(co, (0, 4, 1, 2, 3))[None], rows[:, :, :HEAD_DIM].reshape(db, ATTN_OUT), rows[:, :, HEAD_DIM]


def _sample_ssd_kernel(xbc_ref, z_ref, dtr_ref, sconv_ref, h_ref,
                       convw_ref, convb_ref, dtb_ref, alog_ref, dskip_ref, nw_ref, eexp_ref,
                       y_ref, sconv_out_ref, h_out_ref,
                       uT_ref, decT_ref, bm_ref, cT_ref, xs_ref, yT_ref):
    db = xbc_ref.shape[0]
    bb = h_ref.shape[0]
    i = pl.program_id(0)
    half = SSD_HPG * SSD_HEADDIM
    gn = SSD_GROUPS * SSD_STATE
    cd = SSD_CONV_DIM

    @pl.when(i == 0)
    def _():
        new = xbc_ref[...].astype(F32)
        acc = convb_ref[...] + convw_ref[SSD_CONV - 1:SSD_CONV, :] * new
        for j in range(SSD_CONV - 1):
            acc = acc + convw_ref[j:j + 1, :] * sconv_ref[:, j * cd:(j + 1) * cd]
        for j in range(1, SSD_CONV - 1):
            sconv_out_ref[:, (j - 1) * cd:j * cd] = sconv_ref[:, j * cd:(j + 1) * cd]
        sconv_out_ref[:, (SSD_CONV - 2) * cd:] = new
        xc = acc * _sigmoid(acc)
        xs = xc[:, :SSD_INNER]
        xs_ref[...] = xs
        lane = lax.broadcasted_iota(jnp.int32, (db, LANES), 1)
        dt = jnp.where(lane < SSD_HEADS, _softplus(dtr_ref[...] + dtb_ref[...]), 0.0)
        dec = jnp.exp(dt * (-jnp.exp(alog_ref[...])))
        dt_hi, dt_lo = _split2(dt)
        dt_e = (jnp.dot(dt_hi, eexp_ref[...], preferred_element_type=F32)
                + jnp.dot(dt_lo, eexp_ref[...], preferred_element_type=F32))
        dc_hi, dc_lo = _split2(dec)
        dec_e = (jnp.dot(dc_hi, eexp_ref[...], preferred_element_type=F32)
                 + jnp.dot(dc_lo, eexp_ref[...], preferred_element_type=F32))
        u = dt_e * xs
        for k in range(SSD_INNER // LANES):
            rows = slice(k * LANES, (k + 1) * LANES)
            uT_ref[rows, :] = u[:, rows].T.astype(BF16)
            d_hi, d_lo = _split2(dec_e[:, rows].T)
            decT_ref[rows, 0:db] = d_hi
            decT_ref[rows, db:2 * db] = d_lo
        for g in range(SSD_GROUPS):
            bm_ref[g] = xc[:, SSD_INNER + g * SSD_STATE:SSD_INNER + (g + 1) * SSD_STATE]
            cT_ref[g] = xc[:, SSD_INNER + gn + g * SSD_STATE:SSD_INNER + gn + (g + 1) * SSD_STATE].T
        yT_ref[...] = jnp.zeros_like(yT_ref)

    row_id = lax.broadcasted_iota(jnp.int32, (db, LANES), 0)
    col_id = lax.broadcasted_iota(jnp.int32, (SSD_STATE, db), 1)
    for j in range(bb):
        b = i * bb + j
        on_row = row_id == b
        sel = jnp.where(on_row, 1.0, 0.0).astype(BF16)
        sel2 = jnp.concatenate([sel, sel], axis=0)
        for g in range(SSD_GROUPS):
            rows = slice(g * half, (g + 1) * half)
            rhs_b = jnp.where(on_row, bm_ref[g], 0.0).astype(BF16)
            upd = jnp.dot(uT_ref[rows, :], rhs_b, preferred_element_type=F32)
            dec_rep = jnp.dot(decT_ref[rows, :], sel2, preferred_element_type=F32)
            hn = dec_rep * h_ref[j, rows, :] + upd
            h_out_ref[j, rows, :] = hn
            rhs_c = jnp.where(col_id == b, cT_ref[g], 0.0).astype(BF16)
            yT_ref[rows, :] += jnp.dot(hn.astype(BF16), rhs_c, preferred_element_type=F32)

    @pl.when(i == pl.num_programs(0) - 1)
    def _():
        ss = jnp.zeros((db, 1), F32)
        gated = []
        for k in range(SSD_INNER // LANES):
            cols = slice(k * LANES, (k + 1) * LANES)
            y = yT_ref[cols, :].T + dskip_ref[:, cols] * xs_ref[:, cols]
            zf = z_ref[:, cols].astype(F32)
            gk = y * (zf * _sigmoid(zf))
            gated.append(gk)
            ss = ss + jnp.sum(gk * gk, axis=-1, keepdims=True)
        scale = lax.rsqrt(ss * (1.0 / SSD_INNER) + NORM_EPS)
        for k in range(SSD_INNER // LANES):
            cols = slice(k * LANES, (k + 1) * LANES)
            y_ref[:, cols] = (gated[k] * scale * nw_ref[:, cols]).astype(BF16)


def _sample_ssd(xbc_s, z_s, dt_s, state_conv, state_ssm, conv_w, conv_b, dt_bias128, a_log128,
                dskip_row, ssd_nw):
    db = xbc_s.shape[0]
    assert db == LANES, "sample SSD kernel keeps the sequences on the lane axis"
    bb = SUBLANES
    _, _, eexp = _ssd_consts()
    k = jnp.arange(LANES)
    eexp1 = (jnp.where(k < SSD_HEADS, k, -1)[:, None] == (jnp.arange(SSD_INNER) // SSD_HEADDIM)[None, :]).astype(BF16)
    del eexp
    sconv2 = state_conv.reshape(db, (SSD_CONV - 1) * SSD_CONV_DIM)
    h3 = state_ssm.reshape(db, SSD_INNER, SSD_STATE)
    full = lambda a: _const_spec(a.shape)
    hblk = pl.BlockSpec((bb, SSD_INNER, SSD_STATE), lambda i: (i, 0, 0))
    y, sconv_new, h_new = pl.pallas_call(
        _sample_ssd_kernel,
        grid=(db // bb,),
        in_specs=[full(xbc_s), full(z_s), full(dt_s), full(sconv2), hblk,
                  full(conv_w), full(conv_b), full(dt_bias128), full(a_log128), full(dskip_row),
                  full(ssd_nw), full(eexp1)],
        out_specs=[pl.BlockSpec((db, SSD_INNER), lambda i: (0, 0)),
                   pl.BlockSpec(sconv2.shape, lambda i: (0, 0)), hblk],
        out_shape=[jax.ShapeDtypeStruct((db, SSD_INNER), BF16),
                   jax.ShapeDtypeStruct(sconv2.shape, F32),
                   jax.ShapeDtypeStruct(h3.shape, F32)],
        scratch_shapes=[pltpu.VMEM((SSD_INNER, db), BF16),
                        pltpu.VMEM((SSD_INNER, 2 * db), BF16),
                        pltpu.VMEM((SSD_GROUPS, db, SSD_STATE), F32),
                        pltpu.VMEM((SSD_GROUPS, SSD_STATE, db), F32),
                        pltpu.VMEM((db, SSD_INNER), F32),
                        pltpu.VMEM((SSD_INNER, db), F32)],
        compiler_params=_cparams(("arbitrary",)),
        name="sample_ssd",
    )(xbc_s, z_s, dt_s, sconv2, h3, conv_w, conv_b, dt_bias128, a_log128, dskip_row, ssd_nw, eexp1)
    return y, sconv_new, h_new


def _outproj_kernel(o0_ref, o1_ref, o2_ref, l0_ref, l1_ref, l2_ref, ssm_ref, x_ref,
                    e3_ref, wa_ref, ws_ref, nw_ref, y_ref, ob1_ref, ob2_ref, lb1_ref, lb2_ref):
    tm = x_ref.shape[1]
    hg = ATTN_HG

    def natural(ref, buf_ref):
        dil = ref.shape[1]
        if dil == 1:
            return ref[0, 0].astype(F32)
        chunks = []
        for c in range(ref.shape[3] // LANES):
            for r in range(dil):
                buf_ref[c, pl.ds(r, tm // dil, stride=dil), :] = ref[0, r, :, c * LANES:(c + 1) * LANES].astype(F32)
            chunks.append(buf_ref[c])
        return chunks[0] if len(chunks) == 1 else jnp.concatenate(chunks, axis=1)

    lse = natural(l0_ref, None) + natural(l1_ref, lb1_ref) + natural(l2_ref, lb2_ref)
    lane = lax.broadcasted_iota(jnp.int32, (tm, LANES), 1)
    used = lane < N_PATTERNS * hg

    def over_groups(v, op):
        r = op(op(v, pltpu.roll(v, LANES - hg, 1)), pltpu.roll(v, LANES - 2 * hg, 1))
        return jnp.where(lane < hg, r, jnp.where(lane < 2 * hg, pltpu.roll(r, hg, 1), pltpu.roll(r, 2 * hg, 1)))

    mx = over_groups(lse, jnp.maximum)
    e = jnp.where(used, jnp.exp(lse - mx), 0.0)
    alpha = jnp.where(used, e / over_groups(e, jnp.add), 0.0)
    a_hi, a_lo = _split2(alpha)
    packed = (a_hi.astype(F32) + pltpu.roll(a_lo.astype(F32), 4 * hg, 1)).astype(BF16)
    a_exp = jnp.dot(packed, e3_ref[...], preferred_element_type=F32)
    attn = (a_exp[:, :ATTN_OUT] * natural(o0_ref, None)
            + a_exp[:, ATTN_OUT:2 * ATTN_OUT] * natural(o1_ref, ob1_ref)
            + a_exp[:, 2 * ATTN_OUT:] * natural(o2_ref, ob2_ref))
    mix = (jnp.dot(attn.astype(BF16), wa_ref[...], preferred_element_type=F32)
           + jnp.dot(ssm_ref[0], ws_ref[...], preferred_element_type=F32))
    y_ref[0] = x_ref[0] + _rms_scale(mix) * nw_ref[...]


def _outproj(outs, lses, ssm, x3d, wa, ws, nw, jobs=()):
    b, s, _ = x3d.shape
    assert outs[0].shape[1] == 1
    tm = min(ROW_TILE, s)
    tpb = s // tm
    k = jnp.arange(LANES)
    slot = jnp.where(k % (4 * ATTN_HG) < N_PATTERNS * ATTN_HG, k % (4 * ATTN_HG), -1)
    slot = jnp.where(k < 8 * ATTN_HG, slot, -1)
    e3 = (slot[:, None] == (jnp.arange(N_PATTERNS * ATTN_OUT) // HEAD_DIM)[None, :]).astype(BF16)
    row = lambda w: pl.BlockSpec((1, tm, w), lambda i: (i // tpb, i % tpb, 0))
    grp = lambda a: pl.BlockSpec((1, a.shape[1], tm // a.shape[1], a.shape[3]), lambda i: (i // tpb, 0, i % tpb, 0))
    (y,), cache_res = _row_tile_call(
        _outproj_kernel, jobs,
        grid=(b * tpb,),
        in_specs=[grp(a) for a in outs] + [grp(a) for a in lses] + [row(SSD_INNER), row(D_MODEL),
                  _const_spec(e3.shape), _const_spec(wa.shape), _const_spec(ws.shape), _const_spec(nw.shape)],
        inputs=(*outs, *lses, ssm, x3d, e3, wa, ws, nw),
        out_specs=[row(D_MODEL)],
        out_shape=[jax.ShapeDtypeStruct((b, s, D_MODEL), F32)],
        scratch_shapes=[pltpu.VMEM((ATTN_OUT // LANES, tm, LANES), F32), pltpu.VMEM((ATTN_OUT // LANES, tm, LANES), F32),
                        pltpu.VMEM((1, tm, LANES), F32), pltpu.VMEM((1, tm, LANES), F32)],
        name="outproj")
    return y, cache_res


def _ffn_kernel(y_ref, nw1_ref, wg_ref, wu_ref, wo_ref, nw2_ref, out_ref, xn_ref):
    y = y_ref[...]
    xn_ref[...] = (_rms_scale(y) * nw1_ref[...]).astype(BF16)
    xn = xn_ref[...]
    for j in range(FFN_HIDDEN // FFN_CHUNK):
        cols = slice(j * FFN_CHUNK, (j + 1) * FFN_CHUNK)
        gate = jnp.dot(xn, wg_ref[:, cols], preferred_element_type=F32)
        up = jnp.dot(xn, wu_ref[:, cols], preferred_element_type=F32)
        h = (gate * _sigmoid(gate) * up).astype(BF16)
        part = jnp.dot(h, wo_ref[cols, :], preferred_element_type=F32)
        if j == 0:
            out_ref[...] = part
        else:
            out_ref[...] += part
    out_ref[...] = y + _rms_scale(out_ref[...]) * nw2_ref[...]


def _ffn(y2d, nw1, wg, wu, wo, nw2, jobs=()):
    m = y2d.shape[0]
    tm = min(ROW_TILE, m)
    row = pl.BlockSpec((tm, D_MODEL), lambda i: (i, 0))
    (out,), cache_res = _row_tile_call(
        _ffn_kernel, jobs,
        grid=(m // tm,),
        in_specs=[row, _const_spec(nw1.shape), _const_spec(wg.shape), _const_spec(wu.shape),
                  _const_spec(wo.shape), _const_spec(nw2.shape)],
        inputs=(y2d, nw1, wg, wu, wo, nw2),
        out_specs=[row],
        out_shape=[jax.ShapeDtypeStruct((m, D_MODEL), F32)],
        scratch_shapes=[pltpu.VMEM((tm, D_MODEL), BF16)],
        name="ffn")
    return out, cache_res


def _rope_tables(pos):
    half = HEAD_DIM // 2
    inv_freq = ROPE_THETA ** (-jnp.arange(half, dtype=F32) / half)
    ang = pos.astype(F32)[:, None] * inv_freq[None, :]
    cos, sin = jnp.cos(ang), jnp.sin(ang)
    reps = LANES // HEAD_DIM
    return (jnp.tile(jnp.concatenate([cos, cos], axis=1), (1, reps)),
            jnp.tile(jnp.concatenate([-sin, sin], axis=1), (1, reps)))


def _pad_lanes(v):
    return jnp.pad(v.astype(F32), (0, LANES - v.shape[0])).reshape(1, LANES)


def kernel(x_prompt, x_sample, cache_kv_w128, cache_kv_w512, cache_kv_w2048, state_conv, state_ssm,
           norm_mix_pre, norm_mix_post, norm_ffn_pre, norm_ffn_post, w_in, w_out, conv_w, conv_b,
           dt_bias, a_log, d_skip, ssd_norm_w, w_ffn_in, w_ffn_out):
    b, s, _ = x_prompt.shape
    db, ds, _ = x_sample.shape
    assert ds == 1 and norm_mix_pre.shape[0] == 1, "one layer, one sample token"
    assert s % ROW_TILE == 0 and s % (WIN_KEYS * ATTN_PATTERNS[-1][1]) == 0
    past_len = 8192

    w_in0 = w_in[0]
    o1, o2, o3 = ATTN_QKV, ATTN_QKV + SSD_INNER, ATTN_QKV + SSD_INNER + SSD_CONV_DIM
    wqkv = w_in0[:, :o1].astype(BF16)
    wz = w_in0[:, o1:o2].astype(BF16)
    wxbc = w_in0[:, o2:o3].astype(BF16)
    wdt = jnp.pad(w_in0[:, o3:], ((0, 0), (0, LANES - SSD_HEADS))).astype(BF16)
    wa = w_out[0][:ATTN_OUT].astype(BF16)
    ws = w_out[0][ATTN_OUT:].astype(BF16)
    wg = w_ffn_in[0][:, :FFN_HIDDEN].astype(BF16)
    wu = w_ffn_in[0][:, FFN_HIDDEN:].astype(BF16)
    wo = w_ffn_out[0].astype(BF16)
    nw_mix_pre, nw_mix_post = norm_mix_pre.reshape(1, -1), norm_mix_post.reshape(1, -1)
    nw_ffn_pre, nw_ffn_post = norm_ffn_pre.reshape(1, -1), norm_ffn_post.reshape(1, -1)
    ssd_nw = ssd_norm_w.reshape(1, -1)
    cw, cb = conv_w[0], conv_b.reshape(1, -1)
    dtb, alog = _pad_lanes(dt_bias[0]), _pad_lanes(a_log[0])
    dskip_row = jnp.repeat(d_skip[0].astype(F32), SSD_HEADDIM).reshape(1, SSD_INNER)
    caches = (cache_kv_w128[0], cache_kv_w512[0], cache_kv_w2048[0])

    xs3 = x_sample.reshape(1, db, D_MODEL)
    cos_s, sin_s = _rope_tables(jnp.full((db,), past_len))
    (*qkv_s, z_s, xbc_s, dt_s), _ = _inproj(xs3, nw_mix_pre, cos_s, sin_s, wqkv, wz, wxbc, wdt, (1,) * N_PATTERNS)
    qkv_s = [q.reshape(db, 3 * ATTN_OUT) for q in qkv_s]

    big = N_PATTERNS - 1
    ride_along = b * (s // min(ROW_TILE, s)) == db
    head_slices = {"inproj": (0, 2), "outproj": (2, 2), "ffn": (4, 4)}
    whole = {"inproj": 0, "ffn": 1}
    if ride_along:
        cache_in = [_sample_cache_inputs(qkv_s[g], caches[g]) for g in range(N_PATTERNS)]

    def jobs(key, partial):
        if not ride_along:
            return ()
        c_big, q4_big = cache_in[big]
        res = [(q4_big, c_big, partial, ATTN_PATTERNS[big][1]) + head_slices[key]]
        if key in whole:
            c_g, q4_g = cache_in[whole[key]]
            res.append((q4_g, c_g, None, ATTN_PATTERNS[whole[key]][1], 0, ATTN_HG))
        return res

    dils = tuple(d for _, d in ATTN_PATTERNS)
    cos_p, sin_p = _rope_tables(jnp.arange(s))
    (*qkv_groups, z, xbc, dt_raw), ride0 = _inproj(x_prompt, nw_mix_pre, cos_p, sin_p, wqkv, wz, wxbc, wdt, dils,
                                                    jobs("inproj", None))
    outs, lses = zip(*[_prompt_attention(qkv_groups[g], g) for g in range(N_PATTERNS)])
    ssm, p_conv, h_t = _prompt_ssd(xbc, z, dt_raw, cw, cb, dtb, alog, dskip_row, ssd_nw, b, s)
    y1, ride1 = _outproj(outs, lses, ssm, x_prompt, wa, ws, nw_mix_post,
                         jobs("outproj", ride0[0][0] if ride_along else None))
    y_prompt, ride2 = _ffn(y1.reshape(b * s, D_MODEL), nw_ffn_pre, wg, wu, wo, nw_ffn_post,
                           jobs("ffn", ride1[0][0] if ride_along else None))
    y_prompt = y_prompt.reshape(b, s, D_MODEL)

    p_kv = [_prompt_kv_tail(qkv_groups[g], window, s) for g, (window, _) in enumerate(ATTN_PATTERNS)]
    p_ssm = h_t.reshape(b, SSD_STATE, SSD_HEADS, SSD_HEADDIM).transpose(0, 2, 3, 1)[None]

    s_kv, outs_s, lses_s = [], [], []
    for g in range(N_PATTERNS):
        if g == big and ride_along:
            by_key = {"inproj": ride0, "outproj": ride1, "ffn": ride2}
            rows = jnp.concatenate([by_key[key][0][1][:, 0, :n_heads] for key, (_, n_heads) in
                                    sorted(head_slices.items(), key=lambda kv: kv[1][0])], axis=1)
            kv_new, o_g, lse_g = _sample_cache_outputs(ride2[0][0], rows)
        elif ride_along:
            co, rows = {"inproj": ride0, "ffn": ride2}[[k for k, v in whole.items() if v == g][0]][1]
            kv_new, o_g, lse_g = _sample_cache_outputs(co, rows[:, 0])
        else:
            kv_new, o_g, lse_g = _sample_cache_step(qkv_s[g], caches[g], g)
        s_kv.append(kv_new)
        outs_s.append(o_g.astype(BF16).reshape(1, 1, db, ATTN_OUT))
        lse_g = jnp.pad(lse_g, ((0, 0), (g * ATTN_HG, LANES - (g + 1) * ATTN_HG)))
        lses_s.append(lse_g.reshape(1, 1, db, LANES))
    ssm_s, sconv_new, h_new = _sample_ssd(xbc_s[0], z_s[0], dt_s[0], state_conv[0], state_ssm[0], cw, cb, dtb, alog,
                                          dskip_row, ssd_nw)
    y1_s, _ = _outproj(outs_s, lses_s, ssm_s[None], xs3, wa, ws, nw_mix_post)
    y_sample = _ffn(y1_s[0], nw_ffn_pre, wg, wu, wo, nw_ffn_post)[0].reshape(db, 1, D_MODEL)

    s_conv = sconv_new.reshape(1, db, SSD_CONV - 1, SSD_CONV_DIM)
    s_ssm = h_new.reshape(1, db, SSD_HEADS, SSD_HEADDIM, SSD_STATE)
    return (y_prompt, y_sample, p_kv[0], p_kv[1], p_kv[2], p_conv[None], p_ssm,
            s_kv[0], s_kv[1], s_kv[2], s_conv, s_ssm)
```

```python
import functools
import math

import jax
import jax.numpy as jnp
from jax import lax
from jax.experimental import pallas as pl
from jax.experimental.pallas import tpu as pltpu

F32 = jnp.float32
BF16 = jnp.bfloat16

D_MODEL = 1024
HEAD_DIM = 64
ATTN_PATTERNS = ((128, 1), (512, 4), (2048, 16))
N_PATTERNS = len(ATTN_PATTERNS)
ATTN_HG = 8
WIN_KEYS = 128
ATTN_OUT = ATTN_HG * HEAD_DIM
ATTN_QKV = N_PATTERNS * 3 * ATTN_OUT
ROPE_THETA = 10000.0
ATTN_SCALE = HEAD_DIM ** -0.5
NEG_INF = -1e30
SSD_INNER = D_MODEL
SSD_HEADDIM = 64
SSD_HEADS = SSD_INNER // SSD_HEADDIM
SSD_GROUPS = 2
SSD_HPG = SSD_HEADS // SSD_GROUPS
SSD_STATE = 128
SSD_CONV = 4
SSD_CONV_DIM = SSD_INNER + 2 * SSD_GROUPS * SSD_STATE
FFN_HIDDEN = 2816
NORM_EPS = 1e-6

LANES = 128
SUBLANES = 8
ROW_TILE = 512
SSD_T = 128
SSD_CHUNKS_PER_STEP = 8
FFN_CHUNK = 256
VMEM_LIMIT = 56 * 1024 * 1024
CACHE_BLOCK_BYTES = 4 * 1024 * 1024


def _cparams(sem):
    return pltpu.CompilerParams(dimension_semantics=sem, vmem_limit_bytes=VMEM_LIMIT)


def _const_spec(shape):
    nd = len(shape)
    return pl.BlockSpec(shape, lambda *_: (0,) * nd, pipeline_mode=pl.Buffered(1))


def _split2(x):
    hi = x.astype(BF16)
    lo = (x - hi.astype(F32)).astype(BF16)
    return hi, lo


def _split3(x):
    hi = x.astype(BF16)
    r = x - hi.astype(F32)
    mid = r.astype(BF16)
    lo = (r - mid.astype(F32)).astype(BF16)
    return hi, mid, lo


def _sigmoid(x):
    return 1.0 / (1.0 + jnp.exp(-x))


def _softplus(x):
    return jnp.maximum(x, 0.0) + jnp.log(1.0 + jnp.exp(-jnp.abs(x)))


def _rms_scale(x):
    return x * lax.rsqrt(jnp.mean(x * x, axis=-1, keepdims=True) + NORM_EPS)


def _with_cache_jobs(kernel_fn, n_in, n_out, n_scratch, job_params):
    def wrapped(*refs):
        ins, rest = refs[:n_in], refs[n_in:]
        c_ins = []
        for n_cache_in, _, _, _ in job_params:
            c_ins.append(rest[:n_cache_in])
            rest = rest[n_cache_in:]
        outs, rest = rest[:n_out], rest[n_out:]
        c_outs, rest = rest[:2 * len(job_params)], rest[2 * len(job_params):]
        scratch, col_refs = rest[:n_scratch], rest[n_scratch:]
        for j, (_, dil, head_lo, hv) in enumerate(job_params):
            _sample_cache_body(c_ins[j][0], c_ins[j][1], c_outs[2 * j], c_outs[2 * j + 1], col_refs[j], head_lo, dil, hv)
        kernel_fn(*ins, *outs, *scratch)
    return wrapped


def _row_tile_call(kernel_fn, jobs, *, grid, in_specs, inputs, out_specs, out_shape, scratch_shapes, name):
    in_specs, inputs = list(in_specs), list(inputs)
    out_specs, out_shape, scratch_shapes = list(out_specs), list(out_shape), list(scratch_shapes)
    n_in, n_out, n_scratch = len(in_specs), len(out_specs), len(scratch_shapes)
    job_params, cache_blocks, aliases = [], [], {}
    for q4, cache_t, partial, dil, head_lo, n_heads in jobs:
        db, _, _, hd, wb = cache_t.shape
        assert grid == (db,) and head_lo % n_heads == 0
        cblk = pl.BlockSpec((1, 2, n_heads, hd, wb), lambda i, blk=head_lo // n_heads: (i, 0, blk, 0, 0))
        cache_blocks.append(cblk)
        in_specs += [pl.BlockSpec((1, 3, ATTN_HG, LANES), lambda i: (i, 0, 0, 0)), cblk]
        inputs += [q4, cache_t]
        if partial is not None:
            aliases[len(in_specs)] = n_out + 2 * len(job_params)
            in_specs.append(pl.BlockSpec(memory_space=pl.ANY))
            inputs.append(partial)
        hv = n_heads if n_heads == ATTN_HG and 2 * n_heads * hd * wb * 4 <= CACHE_BLOCK_BYTES // 2 else 1
        job_params.append((2 if partial is None else 3, dil, head_lo, hv))
    for (_, cache_t, *_), cblk in zip(jobs, cache_blocks):
        out_specs += [cblk, pl.BlockSpec((1, 1, SUBLANES, LANES), lambda i: (i, 0, 0, 0))]
        out_shape += [jax.ShapeDtypeStruct(cache_t.shape, F32),
                      jax.ShapeDtypeStruct((cache_t.shape[0], 1, SUBLANES, LANES), F32)]
        scratch_shapes.append(pltpu.VMEM((3, ATTN_HG, cache_t.shape[3], 1), F32))
    body = _with_cache_jobs(kernel_fn, n_in, n_out, n_scratch, job_params) if jobs else kernel_fn
    res = pl.pallas_call(body, grid=grid, in_specs=in_specs, out_specs=out_specs, out_shape=out_shape,
                         scratch_shapes=scratch_shapes, input_output_aliases=aliases,
                         compiler_params=_cparams(("arbitrary",)), name=name)(*inputs)
    return res[:n_out], [(res[n_out + 2 * j], res[n_out + 2 * j + 1]) for j in range(len(jobs))]


def _inproj_kernel(x_ref, nw_ref, cos_ref, sin_ref, wqkv_ref, wz_ref, wxbc_ref, wdt_ref,
                   qkv0_ref, qkv1_ref, qkv2_ref, z_ref, xbc_ref, dt_ref, xn_ref, perm_ref, perm4_ref, *, dils):
    tm = x_ref.shape[1]
    xn_ref[...] = (_rms_scale(x_ref[0]) * nw_ref[...]).astype(BF16)
    xn = xn_ref[...]
    cos = cos_ref[...]
    sin = sin_ref[...]
    lane = lax.broadcasted_iota(jnp.int32, (tm, LANES), 1)
    first_half = (lane % HEAD_DIM) < (HEAD_DIM // 2)
    group_refs = (qkv0_ref, qkv1_ref, qkv2_ref)
    for j in range(ATTN_QKV // ATTN_OUT):
        group, part = divmod(j, 3)
        dil = dils[group]
        out_ref = group_refs[group]
        acc = jnp.dot(xn, wqkv_ref[:, j * ATTN_OUT:(j + 1) * ATTN_OUT], preferred_element_type=F32)
        for c in range(ATTN_OUT // LANES):
            a = acc[:, c * LANES:(c + 1) * LANES]
            if part != 2:
                partner = jnp.where(first_half, pltpu.roll(a, LANES - HEAD_DIM // 2, 1),
                                    pltpu.roll(a, HEAD_DIM // 2, 1))
                a = a * cos + partner * sin
                if part == 0:
                    a = a * ATTN_SCALE
            cols = slice(part * ATTN_OUT + c * LANES, part * ATTN_OUT + (c + 1) * LANES)
            if dil == 1:
                out_ref[0, 0, :, cols] = a.astype(BF16)
            else:
                perm_ref[...] = a
                if dil <= 4:
                    for r in range(dil):
                        out_ref[0, r, :, cols] = perm_ref[pl.ds(r, tm // dil, stride=dil), :].astype(BF16)
                else:
                    for r_lo in range(4):
                        perm4_ref[r_lo] = perm_ref[pl.ds(r_lo, tm // 4, stride=4), :]
                    for r_lo in range(4):
                        for r_hi in range(dil // 4):
                            out_ref[0, 4 * r_hi + r_lo, :, cols] = (
                                perm4_ref[r_lo, pl.ds(r_hi, tm // dil, stride=dil // 4), :].astype(BF16))
    z_ref[0] = jnp.dot(xn, wz_ref[...], preferred_element_type=F32).astype(BF16)
    xbc_ref[0] = jnp.dot(xn, wxbc_ref[...], preferred_element_type=F32).astype(BF16)
    dt_ref[0] = jnp.dot(xn, wdt_ref[...], preferred_element_type=F32)


def _inproj(x3d, nw, cos_t, sin_t, wqkv, wz, wxbc, wdt, dils, jobs=()):
    b, s, _ = x3d.shape
    tm = min(ROW_TILE, s)
    tpb = s // tm
    assert cos_t.shape[0] == s
    row = lambda w: pl.BlockSpec((1, tm, w), lambda i: (i // tpb, i % tpb, 0))
    tab = pl.BlockSpec((tm, LANES), lambda i: (i % tpb, 0))
    grp = lambda d: pl.BlockSpec((1, d, tm // d, 3 * ATTN_OUT), lambda i: (i // tpb, 0, i % tpb, 0))
    return _row_tile_call(
        functools.partial(_inproj_kernel, dils=dils), jobs,
        grid=(b * tpb,),
        in_specs=[row(D_MODEL), _const_spec((1, D_MODEL)), tab, tab,
                  _const_spec(wqkv.shape), _const_spec(wz.shape), _const_spec(wxbc.shape),
                  _const_spec(wdt.shape)],
        inputs=(x3d, nw, cos_t, sin_t, wqkv, wz, wxbc, wdt),
        out_specs=[grp(d) for d in dils] + [row(SSD_INNER), row(SSD_CONV_DIM), row(LANES)],
        out_shape=[jax.ShapeDtypeStruct((b, d, s // d, 3 * ATTN_OUT), BF16) for d in dils]
                  + [jax.ShapeDtypeStruct((b, s, SSD_INNER), BF16),
                     jax.ShapeDtypeStruct((b, s, SSD_CONV_DIM), BF16),
                     jax.ShapeDtypeStruct((b, s, LANES), F32)],
        scratch_shapes=[pltpu.VMEM((tm, D_MODEL), BF16), pltpu.VMEM((tm, LANES), F32),
                        pltpu.VMEM((4, tm // 4, LANES), F32)],
        name="inproj")


def _attn_kernel(q_ref, kc_ref, vc_ref, kp_ref, vp_ref, o_ref, lse_ref, k_scr, v_scr, *, group):
    for rr in range(q_ref.shape[0]):
        _attn_chunk(q_ref.at[rr], kc_ref.at[rr], vc_ref.at[rr], kp_ref.at[rr], vp_ref.at[rr],
                    o_ref.at[rr], lse_ref.at[rr], k_scr.at[rr], v_scr.at[rr], group=group)


def _attn_chunk(q_ref, kc_ref, vc_ref, kp_ref, vp_ref, o_ref, lse_ref, k_scr, v_scr, *, group):
    cq = q_ref.shape[0]
    w = WIN_KEYS
    n = pl.program_id(2)
    k_scr[0:w] = kp_ref[...]
    k_scr[w:] = kc_ref[...]
    v_scr[0:w] = vp_ref[...]
    v_scr[w:] = vc_ref[...]
    lane = lax.broadcasted_iota(jnp.int32, (w, LANES), 1)
    lo = lane < HEAD_DIM
    qi = lax.broadcasted_iota(jnp.int32, (w, 2 * w), 0)
    ki = lax.broadcasted_iota(jnp.int32, (w, 2 * w), 1)
    band = ((ki < w) & (ki >= qi)) | ((ki >= w) & ((ki - w) <= qi))
    for i in range(cq // w):
        if i == 0:
            kmin = jnp.where(n > 0, 0, w)
            valid = band & (ki >= kmin)
        else:
            valid = band
        bias = jnp.where(valid, 0.0, NEG_INF)
        lse_tile = jnp.zeros((w, LANES), F32)
        for pair in range(ATTN_OUT // LANES):
            cols = slice(pair * LANES, (pair + 1) * LANES)
            qp = q_ref[i * w:(i + 1) * w, cols]
            kk = k_scr[i * w:(i + 2) * w, cols]
            vv = v_scr[i * w:(i + 2) * w, cols]
            halves = []
            for half in range(2):
                qh = jnp.where(lo if half == 0 else jnp.logical_not(lo), qp, jnp.zeros_like(qp))
                s = lax.dot_general(qh, kk, (((1,), (1,)), ((), ())), preferred_element_type=F32)
                s = s + bias
                m = jnp.max(s, axis=-1, keepdims=True)
                p = jnp.exp(s - m)
                den = jnp.sum(p, axis=-1, keepdims=True)
                pv = jnp.dot(p.astype(BF16), vv, preferred_element_type=F32)
                halves.append(pv / den)
                head = 2 * pair + half
                lse_tile = jnp.where(lane == group * ATTN_HG + head, m + jnp.log(den), lse_tile)
            o_ref[i * w:(i + 1) * w, cols] = jnp.where(lo, halves[0], halves[1]).astype(BF16)
        lse_ref[i * w:(i + 1) * w, :] = lse_tile


def _prompt_attention(qkv_g, group):
    b, dil, sub_len, _ = qkv_g.shape
    cq = min(ROW_TILE, sub_len)
    blk_per_chunk = cq // WIN_KEYS
    rb = math.gcd(dil, ROW_TILE // cq)
    cur = lambda t: pl.BlockSpec((None, rb, cq, ATTN_OUT), lambda bi, r, n: (bi, r, n, t))
    prev = lambda t: pl.BlockSpec(
        (None, rb, WIN_KEYS, ATTN_OUT),
        lambda bi, r, n: (bi, r, jnp.maximum(n * blk_per_chunk - 1, 0), t))
    out = lambda wdt: pl.BlockSpec((None, rb, cq, wdt), lambda bi, r, n: (bi, r, n, 0))
    return pl.pallas_call(
        functools.partial(_attn_kernel, group=group),
        grid=(b, dil // rb, sub_len // cq),
        in_specs=[cur(0), cur(1), cur(2), prev(1), prev(2)],
        out_specs=[out(ATTN_OUT), out(LANES)],
        out_shape=[jax.ShapeDtypeStruct((b, dil, sub_len, ATTN_OUT), BF16),
                   jax.ShapeDtypeStruct((b, dil, sub_len, LANES), F32)],
        scratch_shapes=[pltpu.VMEM((rb, cq + WIN_KEYS, ATTN_OUT), BF16),
                        pltpu.VMEM((rb, cq + WIN_KEYS, ATTN_OUT), BF16)],
        compiler_params=_cparams(("arbitrary", "arbitrary", "arbitrary")),
        name=f"prompt_attn_g{group}",
    )(qkv_g, qkv_g, qkv_g, qkv_g, qkv_g)


def _pack3_lanes(x):
    hi, mid, lo = _split3(x)
    packed = hi.astype(F32) + pltpu.roll(mid.astype(F32), SSD_HEADS, 1) + pltpu.roll(lo.astype(F32), 2 * SSD_HEADS, 1)
    return packed.astype(BF16)


def _ssd_kernel(xbc_ref, z_ref, dtr_ref, convw_ref, convb_ref, dtb_ref, alog_ref, dskip_ref, nw_ref,
                tri3_ref, rexp_ref, eexp_ref, shift_ref,
                y_ref, pconv_ref, hT_out_ref,
                x2_ref, hT_ref, csb_ref, ybuf_ref):
    t = SSD_T
    n_sub = xbc_ref.shape[1] // t
    c = pl.program_id(1)
    last = pl.num_programs(1) - 1
    taps = SSD_CONV

    @pl.when(c == 0)
    def _():
        x2_ref[0:t] = jnp.zeros((t, SSD_CONV_DIM), BF16)
        hT_ref[...] = jnp.zeros_like(hT_ref)

    lane = lax.broadcasted_iota(jnp.int32, (t, LANES), 1)
    ti = lax.broadcasted_iota(jnp.int32, (t, t), 0)
    si = lax.broadcasted_iota(jnp.int32, (t, t), 1)
    tri = si <= ti
    lo = lane < SSD_HEADDIM
    gn = SSD_GROUPS * SSD_STATE
    pairs_per_group = SSD_HPG // 2
    x2_ref[t:] = xbc_ref[0]

    for sub in range(n_sub):
        rows = slice(sub * t, (sub + 1) * t)
        cur_f = xbc_ref[0, rows].astype(F32)
        acc = convb_ref[...] + convw_ref[taps - 1:taps, :] * cur_f
        for j in range(taps - 1):
            shifted = jnp.dot(shift_ref[j], x2_ref[sub * t:(sub + 2) * t], preferred_element_type=F32)
            acc = acc + convw_ref[j:j + 1, :] * shifted

        if sub == n_sub - 1:
            @pl.when(c == last)
            def _():
                pconv_ref[0] = cur_f[t - (taps - 1):, :]

        xc = acc * _sigmoid(acc)
        xs = xc[:, :SSD_INNER]
        xs_b = xs.astype(BF16)

        dt = jnp.where(lane < SSD_HEADS, _softplus(dtr_ref[0, rows] + dtb_ref[...]), 0.0)
        a = dt * (-jnp.exp(alog_ref[...]))
        a3 = jnp.concatenate(_split3(a), axis=0)
        cs = jnp.dot(tri3_ref[...], a3, preferred_element_type=F32)
        cs_last = cs[t - 1:t, :]
        wl = jnp.exp(cs_last - cs) * dt
        csT = cs.T
        dtT = dt.T
        wlT = wl.T
        csb_ref[sub] = jnp.dot(_pack3_lanes(cs), rexp_ref[...], preferred_element_type=F32)
        cs_last8 = jnp.broadcast_to(cs_last, (SUBLANES, LANES))
        dec = jnp.exp(jnp.dot(_pack3_lanes(cs_last8), eexp_ref[...], preferred_element_type=F32)[0:1, :])

        g_mat, bT, yoff = [], [], []
        for g in range(SSD_GROUPS):
            bm = xc[:, SSD_INNER + g * SSD_STATE:SSD_INNER + (g + 1) * SSD_STATE]
            cm = xc[:, SSD_INNER + gn + g * SSD_STATE:SSD_INNER + gn + (g + 1) * SSD_STATE].astype(BF16)
            g_mat.append(lax.dot_general(cm, bm.astype(BF16), (((1,), (1,)), ((), ())),
                                         preferred_element_type=F32))
            bT.append(bm.T)
            cols = slice(g * SSD_HPG * SSD_HEADDIM, (g + 1) * SSD_HPG * SSD_HEADDIM)
            yoff.append(jnp.dot(cm, hT_ref[:, cols].astype(BF16), preferred_element_type=F32))

        ss = jnp.zeros((t, 1), F32)
        for pair in range(SSD_HEADS // 2):
            g = pair // pairs_per_group
            cols = slice(pair * LANES, (pair + 1) * LANES)
            w_blocks, b_blocks = [], []
            for h in (2 * pair, 2 * pair + 1):
                seg = csb_ref[sub, :, h * LANES:(h + 1) * LANES] - csT[h:h + 1, :]
                seg = jnp.where(tri, seg, NEG_INF)
                w_blocks.append((g_mat[g] * jnp.exp(seg) * dtT[h:h + 1, :]).astype(BF16))
                b_blocks.append((bT[g] * wlT[h:h + 1, :]).astype(BF16))
            lhs = jnp.concatenate([jnp.concatenate(w_blocks, axis=1), jnp.concatenate(b_blocks, axis=1)], axis=0)
            xp = xs_b[:, cols]
            zero = jnp.zeros_like(xp)
            xbd = jnp.concatenate([jnp.where(lo, xp, zero), jnp.where(lo, zero, xp)], axis=0)
            res = jnp.dot(lhs, xbd, preferred_element_type=F32)
            ecs = jnp.exp(jnp.where(lo, csb_ref[sub, :, 2 * pair * LANES:(2 * pair + 1) * LANES],
                                    csb_ref[sub, :, (2 * pair + 1) * LANES:(2 * pair + 2) * LANES]))
            gcol = (pair % pairs_per_group) * LANES
            y = res[:t] + yoff[g][:, gcol:gcol + LANES] * ecs + dskip_ref[:, cols] * xs[:, cols]
            hT_ref[:, cols] = hT_ref[:, cols] * dec[:, cols] + res[t:]
            zf = z_ref[0, rows, cols].astype(F32)
            gated = y * (zf * _sigmoid(zf))
            ybuf_ref[sub, :, cols] = gated
            ss = ss + jnp.sum(gated * gated, axis=-1, keepdims=True)

        y_ref[0, rows] = (ybuf_ref[sub] * lax.rsqrt(ss * (1.0 / SSD_INNER) + NORM_EPS) * nw_ref[...]).astype(BF16)

    x2_ref[0:t] = xbc_ref[0, (n_sub - 1) * t:]

    @pl.when(c == last)
    def _():
        hT_out_ref[0] = hT_ref[...]


def _ssd_consts():
    t = SSD_T
    tri = (jnp.arange(t)[:, None] >= jnp.arange(t)[None, :]).astype(BF16)
    tri3 = jnp.concatenate([tri, tri, tri], axis=1)
    k = jnp.arange(LANES)
    piece_head = jnp.where(k < 3 * SSD_HEADS, k % SSD_HEADS, -1)
    rexp = (piece_head[:, None] == (jnp.arange(SSD_HEADS * LANES) // LANES)[None, :]).astype(BF16)
    eexp = (piece_head[:, None] == (jnp.arange(SSD_INNER) // SSD_HEADDIM)[None, :]).astype(BF16)
    return tri3, rexp, eexp


def _prompt_ssd(xbc, z, dt_raw, conv_w, conv_b, dt_bias128, a_log128, dskip_row, ssd_nw, b, s):
    t = SSD_T
    tri3, rexp, eexp = _ssd_consts()
    back = (SSD_CONV - 1 - jnp.arange(SSD_CONV - 1))[:, None, None]
    shift = (jnp.arange(2 * t)[None, None, :] == t + jnp.arange(t)[None, :, None] - back).astype(BF16)
    n_sub = SSD_CHUNKS_PER_STEP if s % (SSD_CHUNKS_PER_STEP * t) == 0 else 1
    tok = lambda w: pl.BlockSpec((1, n_sub * t, w), lambda bi, c: (bi, c, 0))
    per_b = lambda *shape: pl.BlockSpec((1,) + shape, lambda bi, c: (bi,) + (0,) * len(shape))
    return pl.pallas_call(
        _ssd_kernel,
        grid=(b, s // (n_sub * t)),
        in_specs=[tok(SSD_CONV_DIM), tok(SSD_INNER), tok(LANES),
                  _const_spec(conv_w.shape), _const_spec(conv_b.shape), _const_spec(dt_bias128.shape),
                  _const_spec(a_log128.shape), _const_spec(dskip_row.shape), _const_spec(ssd_nw.shape),
                  _const_spec(tri3.shape), _const_spec(rexp.shape), _const_spec(eexp.shape),
                  _const_spec(shift.shape)],
        out_specs=[tok(SSD_INNER), per_b(SSD_CONV - 1, SSD_CONV_DIM), per_b(SSD_STATE, SSD_INNER)],
        out_shape=[jax.ShapeDtypeStruct((b, s, SSD_INNER), BF16),
                   jax.ShapeDtypeStruct((b, SSD_CONV - 1, SSD_CONV_DIM), F32),
                   jax.ShapeDtypeStruct((b, SSD_STATE, SSD_INNER), F32)],
        scratch_shapes=[pltpu.VMEM(((n_sub + 1) * t, SSD_CONV_DIM), BF16),
                        pltpu.VMEM((SSD_STATE, SSD_INNER), F32),
                        pltpu.VMEM((n_sub, t, SSD_HEADS * LANES), F32),
                        pltpu.VMEM((n_sub, t, SSD_INNER), F32)],
        compiler_params=_cparams(("arbitrary", "arbitrary")),
        name="prompt_ssd",
    )(xbc.reshape(b, s, SSD_CONV_DIM), z.reshape(b, s, SSD_INNER), dt_raw.reshape(b, s, LANES),
      conv_w, conv_b, dt_bias128, a_log128, dskip_row, ssd_nw, tri3, rexp, eexp, shift)


def _pkv_kernel(k_ref, v_ref, out_ref, buf_ref):
    _, dil, rows, _ = k_ref.shape
    tb = dil * rows
    for part, ref in enumerate((k_ref, v_ref)):
        for c in range(ATTN_OUT // LANES):
            cols = slice(c * LANES, (c + 1) * LANES)
            if dil == 1:
                nat = ref[0, 0, :, cols].astype(F32)
            else:
                for r in range(dil):
                    buf_ref[pl.ds(r, rows, stride=dil), :] = ref[0, r, :, cols].astype(F32)
                nat = buf_ref[...]
            base = part * ATTN_OUT + c * LANES
            for tblk in range(tb // LANES):
                out_ref[0, base:base + LANES, tblk * LANES:(tblk + 1) * LANES] = nat[tblk * LANES:(tblk + 1) * LANES, :].T


def _prompt_kv_tail(qkv_g, window, s):
    b, dil, _, _ = qkv_g.shape
    wlen = min(window, s)
    tb = min(ROW_TILE, wlen)
    assert (s - wlen) % tb == 0 and tb % (dil * 2 * SUBLANES) == 0
    first = (s - wlen) // tb
    blk = lambda t: pl.BlockSpec((1, dil, tb // dil, ATTN_OUT), lambda bi, n: (bi, 0, first + n, t))
    out = pl.pallas_call(
        _pkv_kernel,
        grid=(b, wlen // tb),
        in_specs=[blk(1), blk(2)],
        out_specs=pl.BlockSpec((1, 2 * ATTN_OUT, tb), lambda bi, n: (bi, 0, n)),
        out_shape=jax.ShapeDtypeStruct((b, 2 * ATTN_OUT, wlen), F32),
        scratch_shapes=[pltpu.VMEM((tb, LANES), F32)],
        compiler_params=_cparams(("arbitrary", "arbitrary")),
        name=f"prompt_kv_tail_w{window}",
    )(qkv_g, qkv_g)
    return jnp.transpose(out.reshape(1, b, 2, ATTN_HG, HEAD_DIM, wlen), (0, 1, 5, 2, 3, 4))


def _sample_cache_kernel(q_ref, c_ref, co_ref, o_ref, col_ref, *, dil, hv):
    _sample_cache_body(q_ref, c_ref, co_ref, o_ref, col_ref, pl.program_id(1) * c_ref.shape[2], dil, hv)


def _sample_cache_body(q_ref, c_ref, co_ref, o_ref, col_ref, head0, dil, hv):
    bb, _, hb, hd, wb = c_ref.shape
    pos = lax.broadcasted_iota(jnp.int32, (1, 1, wb), 2)
    in_window = (pos & (dil - 1)) == 0
    is_last = lax.broadcasted_iota(jnp.int32, (1, hd, wb), 2) == wb - 1
    lane = lax.broadcasted_iota(jnp.int32, (hd, LANES), 1)
    lane_row = lax.broadcasted_iota(jnp.int32, (1, LANES), 1)
    for bi in range(bb):
        for part in range(3):
            cols = q_ref[bi, part].T
            for h in range(ATTN_HG):
                col_ref[part, h] = cols[0:hd, h:h + 1]
        o_cols = jnp.zeros((hd, LANES), F32)
        lse_row = jnp.zeros((1, LANES), F32)
        for h in range(0, hb, hv):
            q = col_ref[0, pl.ds(head0 + h, hv)]
            k_new = col_ref[1, pl.ds(head0 + h, hv)]
            v_new = col_ref[2, pl.ds(head0 + h, hv)]
            keys = c_ref[bi, 0, h:h + hv]
            vals = c_ref[bi, 1, h:h + hv]
            sc = jnp.where(in_window, jnp.sum(keys * q, axis=1, keepdims=True), NEG_INF)
            sc_new = jnp.sum(k_new * q, axis=1, keepdims=True)
            m = jnp.maximum(jnp.max(sc, axis=2, keepdims=True), sc_new)
            p = jnp.exp(sc - m)
            p_new = jnp.exp(sc_new - m)
            den = jnp.sum(p, axis=2, keepdims=True) + p_new
            out = (jnp.sum(vals * p, axis=2, keepdims=True) + v_new * p_new) / den
            lse = m + jnp.log(den)
            for t in range(hv):
                o_cols = jnp.where(lane == h + t, out[t], o_cols)
                lse_row = jnp.where(lane_row == h + t, lse[t], lse_row)
            co_ref[bi, 0, h:h + hv] = jnp.where(is_last, k_new, pltpu.roll(keys, wb - 1, 2))
            co_ref[bi, 1, h:h + hv] = jnp.where(is_last, v_new, pltpu.roll(vals, wb - 1, 2))
        tile = jnp.concatenate([o_cols, jnp.broadcast_to(lse_row, (SUBLANES, LANES)),
                                jnp.zeros((LANES - hd - SUBLANES, LANES), F32)], axis=0)
        o_ref[bi, 0] = tile.T[0:SUBLANES, :]


def _sample_cache_step(qkv_g, cache, group):
    db = qkv_g.shape[0]
    window, dil = ATTN_PATTERNS[group]
    wb = cache.shape[1]
    assert wb == window and dil & (dil - 1) == 0, "sample path expects a full window of cached rows"
    c_t, q4 = _sample_cache_inputs(qkv_g, cache)
    row_bytes = 2 * HEAD_DIM * wb * 4
    hb = max(1, min(ATTN_HG, CACHE_BLOCK_BYTES // row_bytes))
    bb = max(1, min(SUBLANES, CACHE_BLOCK_BYTES // (row_bytes * ATTN_HG)))
    hv = hb if hb == ATTN_HG else 1
    n_hblk = ATTN_HG // hb
    cblk = pl.BlockSpec((bb, 2, hb, HEAD_DIM, wb), lambda i, j: (i, 0, j, 0, 0))
    co, o = pl.pallas_call(
        functools.partial(_sample_cache_kernel, dil=dil, hv=hv),
        grid=(db // bb, n_hblk),
        in_specs=[pl.BlockSpec((bb, 3, ATTN_HG, LANES), lambda i, j: (i, 0, 0, 0)), cblk],
        out_specs=[cblk, pl.BlockSpec((bb, 1, SUBLANES, LANES), lambda i, j: (i, j, 0, 0))],
        out_shape=[jax.ShapeDtypeStruct(c_t.shape, F32),
                   jax.ShapeDtypeStruct((db, n_hblk, SUBLANES, LANES), F32)],
        scratch_shapes=[pltpu.VMEM((3, ATTN_HG, HEAD_DIM, 1), F32)],
        compiler_params=_cparams(("arbitrary", "arbitrary")),
        name=f"sample_cache_g{group}",
    )(q4, c_t)
    return _sample_cache_outputs(co, o[:, :, :hb].reshape(db, ATTN_HG, LANES))


def _sample_cache_inputs(qkv_g, cache):
    db = qkv_g.shape[0]
    c_t = jnp.transpose(cache, (0, 2, 3, 4, 1))
    q4 = jnp.pad(qkv_g.astype(F32).reshape(db, 3, ATTN_HG, HEAD_DIM), ((0, 0), (0, 0), (0, 0), (0, LANES - HEAD_DIM)))
    return c_t, q4


def _sample_cache_outputs(co, rows):
    db = rows.shape[0]
    return jnp.transpose(co, (0, 4, 1, 2, 3))[None], rows[:, :, :HEAD_DIM].reshape(db, ATTN_OUT), rows[:, :, HEAD_DIM]


def _sample_ssd_kernel(xbc_ref, z_ref, dtr_ref, sconv_ref, h_ref,
                       convw_ref, convb_ref, dtb_ref, alog_ref, dskip_ref, nw_ref, eexp_ref,
                       y_ref, sconv_out_ref, h_out_ref,
                       uT_ref, decT_ref, bm_ref, cT_ref, xs_ref, yT_ref):
    db = xbc_ref.shape[0]
    bb = h_ref.shape[0]
    i = pl.program_id(0)
    half = SSD_HPG * SSD_HEADDIM
    gn = SSD_GROUPS * SSD_STATE
    cd = SSD_CONV_DIM

    @pl.when(i == 0)
    def _():
        new = xbc_ref[...].astype(F32)
        acc = convb_ref[...] + convw_ref[SSD_CONV - 1:SSD_CONV, :] * new
        for j in range(SSD_CONV - 1):
            acc = acc + convw_ref[j:j + 1, :] * sconv_ref[:, j * cd:(j + 1) * cd]
        for j in range(1, SSD_CONV - 1):
            sconv_out_ref[:, (j - 1) * cd:j * cd] = sconv_ref[:, j * cd:(j + 1) * cd]
        sconv_out_ref[:, (SSD_CONV - 2) * cd:] = new
        xc = acc * _sigmoid(acc)
        xs = xc[:, :SSD_INNER]
        xs_ref[...] = xs
        lane = lax.broadcasted_iota(jnp.int32, (db, LANES), 1)
        dt = jnp.where(lane < SSD_HEADS, _softplus(dtr_ref[...] + dtb_ref[...]), 0.0)
        dec = jnp.exp(dt * (-jnp.exp(alog_ref[...])))
        dt_hi, dt_lo = _split2(dt)
        dt_e = (jnp.dot(dt_hi, eexp_ref[...], preferred_element_type=F32)
                + jnp.dot(dt_lo, eexp_ref[...], preferred_element_type=F32))
        dc_hi, dc_lo = _split2(dec)
        dec_e = (jnp.dot(dc_hi, eexp_ref[...], preferred_element_type=F32)
                 + jnp.dot(dc_lo, eexp_ref[...], preferred_element_type=F32))
        u = dt_e * xs
        for k in range(SSD_INNER // LANES):
            rows = slice(k * LANES, (k + 1) * LANES)
            uT_ref[rows, :] = u[:, rows].T.astype(BF16)
            d_hi, d_lo = _split2(dec_e[:, rows].T)
            decT_ref[rows, 0:db] = d_hi
            decT_ref[rows, db:2 * db] = d_lo
        for g in range(SSD_GROUPS):
            bm_ref[g] = xc[:, SSD_INNER + g * SSD_STATE:SSD_INNER + (g + 1) * SSD_STATE]
            cT_ref[g] = xc[:, SSD_INNER + gn + g * SSD_STATE:SSD_INNER + gn + (g + 1) * SSD_STATE].T
        yT_ref[...] = jnp.zeros_like(yT_ref)

    row_id = lax.broadcasted_iota(jnp.int32, (db, LANES), 0)
    col_id = lax.broadcasted_iota(jnp.int32, (SSD_STATE, db), 1)
    for j in range(bb):
        b = i * bb + j
        on_row = row_id == b
        sel = jnp.where(on_row, 1.0, 0.0).astype(BF16)
        sel2 = jnp.concatenate([sel, sel], axis=0)
        for g in range(SSD_GROUPS):
            rows = slice(g * half, (g + 1) * half)
            rhs_b = jnp.where(on_row, bm_ref[g], 0.0).astype(BF16)
            upd = jnp.dot(uT_ref[rows, :], rhs_b, preferred_element_type=F32)
            dec_rep = jnp.dot(decT_ref[rows, :], sel2, preferred_element_type=F32)
            hn = dec_rep * h_ref[j, rows, :] + upd
            h_out_ref[j, rows, :] = hn
            rhs_c = jnp.where(col_id == b, cT_ref[g], 0.0).astype(BF16)
            yT_ref[rows, :] += jnp.dot(hn.astype(BF16), rhs_c, preferred_element_type=F32)

    @pl.when(i == pl.num_programs(0) - 1)
    def _():
        ss = jnp.zeros((db, 1), F32)
        gated = []
        for k in range(SSD_INNER // LANES):
            cols = slice(k * LANES, (k + 1) * LANES)
            y = yT_ref[cols, :].T + dskip_ref[:, cols] * xs_ref[:, cols]
            zf = z_ref[:, cols].astype(F32)
            gk = y * (zf * _sigmoid(zf))
            gated.append(gk)
            ss = ss + jnp.sum(gk * gk, axis=-1, keepdims=True)
        scale = lax.rsqrt(ss * (1.0 / SSD_INNER) + NORM_EPS)
        for k in range(SSD_INNER // LANES):
            cols = slice(k * LANES, (k + 1) * LANES)
            y_ref[:, cols] = (gated[k] * scale * nw_ref[:, cols]).astype(BF16)


def _sample_ssd(xbc_s, z_s, dt_s, state_conv, state_ssm, conv_w, conv_b, dt_bias128, a_log128,
                dskip_row, ssd_nw):
    db = xbc_s.shape[0]
    assert db == LANES, "sample SSD kernel keeps the sequences on the lane axis"
    bb = SUBLANES
    _, _, eexp = _ssd_consts()
    k = jnp.arange(LANES)
    eexp1 = (jnp.where(k < SSD_HEADS, k, -1)[:, None] == (jnp.arange(SSD_INNER) // SSD_HEADDIM)[None, :]).astype(BF16)
    del eexp
    sconv2 = state_conv.reshape(db, (SSD_CONV - 1) * SSD_CONV_DIM)
    h3 = state_ssm.reshape(db, SSD_INNER, SSD_STATE)
    full = lambda a: _const_spec(a.shape)
    hblk = pl.BlockSpec((bb, SSD_INNER, SSD_STATE), lambda i: (i, 0, 0))
    y, sconv_new, h_new = pl.pallas_call(
        _sample_ssd_kernel,
        grid=(db // bb,),
        in_specs=[full(xbc_s), full(z_s), full(dt_s), full(sconv2), hblk,
                  full(conv_w), full(conv_b), full(dt_bias128), full(a_log128), full(dskip_row),
                  full(ssd_nw), full(eexp1)],
        out_specs=[pl.BlockSpec((db, SSD_INNER), lambda i: (0, 0)),
                   pl.BlockSpec(sconv2.shape, lambda i: (0, 0)), hblk],
        out_shape=[jax.ShapeDtypeStruct((db, SSD_INNER), BF16),
                   jax.ShapeDtypeStruct(sconv2.shape, F32),
                   jax.ShapeDtypeStruct(h3.shape, F32)],
        scratch_shapes=[pltpu.VMEM((SSD_INNER, db), BF16),
                        pltpu.VMEM((SSD_INNER, 2 * db), BF16),
                        pltpu.VMEM((SSD_GROUPS, db, SSD_STATE), F32),
                        pltpu.VMEM((SSD_GROUPS, SSD_STATE, db), F32),
                        pltpu.VMEM((db, SSD_INNER), F32),
                        pltpu.VMEM((SSD_INNER, db), F32)],
        compiler_params=_cparams(("arbitrary",)),
        name="sample_ssd",
    )(xbc_s, z_s, dt_s, sconv2, h3, conv_w, conv_b, dt_bias128, a_log128, dskip_row, ssd_nw, eexp1)
    return y, sconv_new, h_new


def _outproj_kernel(o0_ref, o1_ref, o2_ref, l0_ref, l1_ref, l2_ref, ssm_ref, x_ref,
                    e3_ref, wa_ref, ws_ref, nw_ref, y_ref, ob1_ref, ob2_ref, lb1_ref, lb2_ref, mid_ref):
    tm = x_ref.shape[1]
    hg = ATTN_HG

    def natural(ref, buf_ref):
        dil = ref.shape[1]
        if dil == 1:
            return ref[0, 0].astype(F32)
        chunks = []
        for c in range(ref.shape[3] // LANES):
            if dil <= 4:
                for r in range(dil):
                    buf_ref[c, pl.ds(r, tm // dil, stride=dil), :] = ref[0, r, :, c * LANES:(c + 1) * LANES].astype(F32)
            else:
                for r_lo in range(4):
                    for r_hi in range(dil // 4):
                        mid_ref[r_lo, pl.ds(r_hi, tm // dil, stride=dil // 4), :] = (
                            ref[0, 4 * r_hi + r_lo, :, c * LANES:(c + 1) * LANES].astype(F32))
                for r_lo in range(4):
                    buf_ref[c, pl.ds(r_lo, tm // 4, stride=4), :] = mid_ref[r_lo]
            chunks.append(buf_ref[c])
        return chunks[0] if len(chunks) == 1 else jnp.concatenate(chunks, axis=1)

    lse = natural(l0_ref, None) + natural(l1_ref, lb1_ref) + natural(l2_ref, lb2_ref)
    lane = lax.broadcasted_iota(jnp.int32, (tm, LANES), 1)
    used = lane < N_PATTERNS * hg

    def over_groups(v, op):
        r = op(op(v, pltpu.roll(v, LANES - hg, 1)), pltpu.roll(v, LANES - 2 * hg, 1))
        return jnp.where(lane < hg, r, jnp.where(lane < 2 * hg, pltpu.roll(r, hg, 1), pltpu.roll(r, 2 * hg, 1)))

    mx = over_groups(lse, jnp.maximum)
    e = jnp.where(used, jnp.exp(lse - mx), 0.0)
    alpha = jnp.where(used, e / over_groups(e, jnp.add), 0.0)
    a_hi, a_lo = _split2(alpha)
    packed = (a_hi.astype(F32) + pltpu.roll(a_lo.astype(F32), 4 * hg, 1)).astype(BF16)
    a_exp = jnp.dot(packed, e3_ref[...], preferred_element_type=F32)
    attn = (a_exp[:, :ATTN_OUT] * natural(o0_ref, None)
            + a_exp[:, ATTN_OUT:2 * ATTN_OUT] * natural(o1_ref, ob1_ref)
            + a_exp[:, 2 * ATTN_OUT:] * natural(o2_ref, ob2_ref))
    mix = (jnp.dot(attn.astype(BF16), wa_ref[...], preferred_element_type=F32)
           + jnp.dot(ssm_ref[0], ws_ref[...], preferred_element_type=F32))
    y_ref[0] = x_ref[0] + _rms_scale(mix) * nw_ref[...]


def _outproj(outs, lses, ssm, x3d, wa, ws, nw, jobs=()):
    b, s, _ = x3d.shape
    assert outs[0].shape[1] == 1
    tm = min(ROW_TILE, s)
    tpb = s // tm
    k = jnp.arange(LANES)
    slot = jnp.where(k % (4 * ATTN_HG) < N_PATTERNS * ATTN_HG, k % (4 * ATTN_HG), -1)
    slot = jnp.where(k < 8 * ATTN_HG, slot, -1)
    e3 = (slot[:, None] == (jnp.arange(N_PATTERNS * ATTN_OUT) // HEAD_DIM)[None, :]).astype(BF16)
    row = lambda w: pl.BlockSpec((1, tm, w), lambda i: (i // tpb, i % tpb, 0))
    grp = lambda a: pl.BlockSpec((1, a.shape[1], tm // a.shape[1], a.shape[3]), lambda i: (i // tpb, 0, i % tpb, 0))
    (y,), cache_res = _row_tile_call(
        _outproj_kernel, jobs,
        grid=(b * tpb,),
        in_specs=[grp(a) for a in outs] + [grp(a) for a in lses] + [row(SSD_INNER), row(D_MODEL),
                  _const_spec(e3.shape), _const_spec(wa.shape), _const_spec(ws.shape), _const_spec(nw.shape)],
        inputs=(*outs, *lses, ssm, x3d, e3, wa, ws, nw),
        out_specs=[row(D_MODEL)],
        out_shape=[jax.ShapeDtypeStruct((b, s, D_MODEL), F32)],
        scratch_shapes=[pltpu.VMEM((ATTN_OUT // LANES, tm, LANES), F32), pltpu.VMEM((ATTN_OUT // LANES, tm, LANES), F32),
                        pltpu.VMEM((1, tm, LANES), F32), pltpu.VMEM((1, tm, LANES), F32),
                        pltpu.VMEM((4, max(tm // 4, SUBLANES), LANES), F32)],
        name="outproj")
    return y, cache_res


def _ffn_kernel(y_ref, nw1_ref, wg_ref, wu_ref, wo_ref, nw2_ref, out_ref, xn_ref):
    y = y_ref[...]
    xn_ref[...] = (_rms_scale(y) * nw1_ref[...]).astype(BF16)
    xn = xn_ref[...]
    for j in range(FFN_HIDDEN // FFN_CHUNK):
        cols = slice(j * FFN_CHUNK, (j + 1) * FFN_CHUNK)
        gate = jnp.dot(xn, wg_ref[:, cols], preferred_element_type=F32)
        up = jnp.dot(xn, wu_ref[:, cols], preferred_element_type=F32)
        h = (gate * _sigmoid(gate) * up).astype(BF16)
        part = jnp.dot(h, wo_ref[cols, :], preferred_element_type=F32)
        if j == 0:
            out_ref[...] = part
        else:
            out_ref[...] += part
    out_ref[...] = y + _rms_scale(out_ref[...]) * nw2_ref[...]


def _ffn(y2d, nw1, wg, wu, wo, nw2, jobs=()):
    m = y2d.shape[0]
    tm = min(ROW_TILE, m)
    row = pl.BlockSpec((tm, D_MODEL), lambda i: (i, 0))
    (out,), cache_res = _row_tile_call(
        _ffn_kernel, jobs,
        grid=(m // tm,),
        in_specs=[row, _const_spec(nw1.shape), _const_spec(wg.shape), _const_spec(wu.shape),
                  _const_spec(wo.shape), _const_spec(nw2.shape)],
        inputs=(y2d, nw1, wg, wu, wo, nw2),
        out_specs=[row],
        out_shape=[jax.ShapeDtypeStruct((m, D_MODEL), F32)],
        scratch_shapes=[pltpu.VMEM((tm, D_MODEL), BF16)],
        name="ffn")
    return out, cache_res


def _rope_tables(pos):
    half = HEAD_DIM // 2
    inv_freq = ROPE_THETA ** (-jnp.arange(half, dtype=F32) / half)
    ang = pos.astype(F32)[:, None] * inv_freq[None, :]
    cos, sin = jnp.cos(ang), jnp.sin(ang)
    reps = LANES // HEAD_DIM
    return (jnp.tile(jnp.concatenate([cos, cos], axis=1), (1, reps)),
            jnp.tile(jnp.concatenate([-sin, sin], axis=1), (1, reps)))


def _pad_lanes(v):
    return jnp.pad(v.astype(F32), (0, LANES - v.shape[0])).reshape(1, LANES)


def kernel(x_prompt, x_sample, cache_kv_w128, cache_kv_w512, cache_kv_w2048, state_conv, state_ssm,
           norm_mix_pre, norm_mix_post, norm_ffn_pre, norm_ffn_post, w_in, w_out, conv_w, conv_b,
           dt_bias, a_log, d_skip, ssd_norm_w, w_ffn_in, w_ffn_out):
    b, s, _ = x_prompt.shape
    db, ds, _ = x_sample.shape
    assert ds == 1 and norm_mix_pre.shape[0] == 1, "one layer, one sample token"
    assert s % ROW_TILE == 0 and s % (WIN_KEYS * ATTN_PATTERNS[-1][1]) == 0
    past_len = 8192

    w_in0 = w_in[0]
    o1, o2, o3 = ATTN_QKV, ATTN_QKV + SSD_INNER, ATTN_QKV + SSD_INNER + SSD_CONV_DIM
    wqkv = w_in0[:, :o1].astype(BF16)
    wz = w_in0[:, o1:o2].astype(BF16)
    wxbc = w_in0[:, o2:o3].astype(BF16)
    wdt = jnp.pad(w_in0[:, o3:], ((0, 0), (0, LANES - SSD_HEADS))).astype(BF16)
    wa = w_out[0][:ATTN_OUT].astype(BF16)
    ws = w_out[0][ATTN_OUT:].astype(BF16)
    wg = w_ffn_in[0][:, :FFN_HIDDEN].astype(BF16)
    wu = w_ffn_in[0][:, FFN_HIDDEN:].astype(BF16)
    wo = w_ffn_out[0].astype(BF16)
    nw_mix_pre, nw_mix_post = norm_mix_pre.reshape(1, -1), norm_mix_post.reshape(1, -1)
    nw_ffn_pre, nw_ffn_post = norm_ffn_pre.reshape(1, -1), norm_ffn_post.reshape(1, -1)
    ssd_nw = ssd_norm_w.reshape(1, -1)
    cw, cb = conv_w[0], conv_b.reshape(1, -1)
    dtb, alog = _pad_lanes(dt_bias[0]), _pad_lanes(a_log[0])
    dskip_row = jnp.repeat(d_skip[0].astype(F32), SSD_HEADDIM).reshape(1, SSD_INNER)
    caches = (cache_kv_w128[0], cache_kv_w512[0], cache_kv_w2048[0])

    xs3 = x_sample.reshape(1, db, D_MODEL)
    cos_s, sin_s = _rope_tables(jnp.full((db,), past_len))
    (*qkv_s, z_s, xbc_s, dt_s), _ = _inproj(xs3, nw_mix_pre, cos_s, sin_s, wqkv, wz, wxbc, wdt, (1,) * N_PATTERNS)
    qkv_s = [q.reshape(db, 3 * ATTN_OUT) for q in qkv_s]

    big = N_PATTERNS - 1
    ride_along = b * (s // min(ROW_TILE, s)) == db
    head_slices = {"inproj": (0, 2), "outproj": (2, 2), "ffn": (4, 4)}
    whole = {"inproj": 0, "ffn": 1}
    if ride_along:
        cache_in = [_sample_cache_inputs(qkv_s[g], caches[g]) for g in range(N_PATTERNS)]

    def jobs(key, partial):
        if not ride_along:
            return ()
        c_big, q4_big = cache_in[big]
        res = [(q4_big, c_big, partial, ATTN_PATTERNS[big][1]) + head_slices[key]]
        if key in whole:
            c_g, q4_g = cache_in[whole[key]]
            res.append((q4_g, c_g, None, ATTN_PATTERNS[whole[key]][1], 0, ATTN_HG))
        return res

    dils = tuple(d for _, d in ATTN_PATTERNS)
    cos_p, sin_p = _rope_tables(jnp.arange(s))
    (*qkv_groups, z, xbc, dt_raw), ride0 = _inproj(x_prompt, nw_mix_pre, cos_p, sin_p, wqkv, wz, wxbc, wdt, dils,
                                                    jobs("inproj", None))
    outs, lses = zip(*[_prompt_attention(qkv_groups[g], g) for g in range(N_PATTERNS)])
    ssm, p_conv, h_t = _prompt_ssd(xbc, z, dt_raw, cw, cb, dtb, alog, dskip_row, ssd_nw, b, s)
    y1, ride1 = _outproj(outs, lses, ssm, x_prompt, wa, ws, nw_mix_post,
                         jobs("outproj", ride0[0][0] if ride_along else None))
    y_prompt, ride2 = _ffn(y1.reshape(b * s, D_MODEL), nw_ffn_pre, wg, wu, wo, nw_ffn_post,
                           jobs("ffn", ride1[0][0] if ride_along else None))
    y_prompt = y_prompt.reshape(b, s, D_MODEL)

    p_kv = [_prompt_kv_tail(qkv_groups[g], window, s) for g, (window, _) in enumerate(ATTN_PATTERNS)]
    p_ssm = h_t.reshape(b, SSD_STATE, SSD_HEADS, SSD_HEADDIM).transpose(0, 2, 3, 1)[None]

    s_kv, outs_s, lses_s = [], [], []
    for g in range(N_PATTERNS):
        if g == big and ride_along:
            by_key = {"inproj": ride0, "outproj": ride1, "ffn": ride2}
            rows = jnp.concatenate([by_key[key][0][1][:, 0, :n_heads] for key, (_, n_heads) in
                                    sorted(head_slices.items(), key=lambda kv: kv[1][0])], axis=1)
            kv_new, o_g, lse_g = _sample_cache_outputs(ride2[0][0], rows)
        elif ride_along:
            co, rows = {"inproj": ride0, "ffn": ride2}[[k for k, v in whole.items() if v == g][0]][1]
            kv_new, o_g, lse_g = _sample_cache_outputs(co, rows[:, 0])
        else:
            kv_new, o_g, lse_g = _sample_cache_step(qkv_s[g], caches[g], g)
        s_kv.append(kv_new)
        outs_s.append(o_g.astype(BF16).reshape(1, 1, db, ATTN_OUT))
        lse_g = jnp.pad(lse_g, ((0, 0), (g * ATTN_HG, LANES - (g + 1) * ATTN_HG)))
        lses_s.append(lse_g.reshape(1, 1, db, LANES))
    ssm_s, sconv_new, h_new = _sample_ssd(xbc_s[0], z_s[0], dt_s[0], state_conv[0], state_ssm[0], cw, cb, dtb, alog,
                                          dskip_row, ssd_nw)
    y1_s, _ = _outproj(outs_s, lses_s, ssm_s[None], xs3, wa, ws, nw_mix_post)
    y_sample = _ffn(y1_s[0], nw_ffn_pre, wg, wu, wo, nw_ffn_post)[0].reshape(db, 1, D_MODEL)

    s_conv = sconv_new.reshape(1, db, SSD_CONV - 1, SSD_CONV_DIM)
    s_ssm = h_new.reshape(1, db, SSD_HEADS, SSD_HEADDIM, SSD_STATE)
    return (y_prompt, y_sample, p_kv[0], p_kv[1], p_kv[2], p_conv[None], p_ssm,
            s_kv[0], s_kv[1], s_kv[2], s_conv, s_ssm)
```
